```python
import functools
import jax, jax.numpy as jnp
from jax import lax
import numpy as np

D_MODEL = 1024
BATCH = 16
SEQ = 2048
DEPTH = 2
DEC_BATCH = 128
DEC_SEQ = 4
PAST_LEN = 16384
PAGE_SIZE = 128

SGU_CHUNK = 128
SGU_GROUPS = 4
SGU_GROUP_CH = 128
SGU_WIDTH = SGU_GROUPS * SGU_GROUP_CH
MLSTM_HEADS = 4
MLSTM_DK = 128
MLSTM_DV = 128
MLSTM_CHUNK = 128
FORGET_BIAS = 3.0
MLA_HEADS = 8
MLA_Q_RANK = 256
MLA_KV_RANK = 256
MLA_NOPE = 64
MLA_ROPE = 32
MLA_QK = MLA_NOPE + MLA_ROPE
MLA_V = 64
MLA_Q_BLOCK = 128
ROPE_BASE = 10000.0
N_BRANCH = 3
BRANCH_WIDTH = 512
IN_SIZES = (SGU_WIDTH, SGU_WIDTH, MLSTM_HEADS * MLSTM_DK, MLSTM_HEADS * MLSTM_DK, MLSTM_HEADS * MLSTM_DV, MLSTM_HEADS * MLSTM_DV, MLSTM_HEADS, MLSTM_HEADS, MLA_Q_RANK, MLA_KV_RANK, MLA_ROPE)
N_IN = 2 * SGU_WIDTH + 2 * MLSTM_HEADS * MLSTM_DK + 2 * MLSTM_HEADS * MLSTM_DV + 2 * MLSTM_HEADS + MLA_Q_RANK + MLA_KV_RANK + MLA_ROPE
FORGET_COL = 2 * SGU_WIDTH + 2 * MLSTM_HEADS * MLSTM_DK + 2 * MLSTM_HEADS * MLSTM_DV + MLSTM_HEADS
N_EXPERTS = 16
N_EXPERT_GROUPS = 4
EXPERTS_PER_GROUP = 4
TOP_K = 2
D_EXPERT = 256
EPS = 1e-6

kernel_name = 'hybrid_sgu_mlstm_mla_moe_step'


def _rmsnorm(x, g):
    xf = x.astype(jnp.float32)
    y = xf * lax.rsqrt(jnp.mean(xf * xf, axis=-1, keepdims=True) + EPS)
    return (y * g.astype(jnp.float32)).astype(x.dtype)


def _rope(x, pos):
    half = MLA_ROPE // 2
    freqs = ROPE_BASE ** (-jnp.arange(half, dtype=jnp.float32) / half)
    ang = pos[:, None] * freqs[None, :]
    shape = (ang.shape[0],) + (1,) * (x.ndim - 3) + (half,)
    cos = jnp.cos(ang).reshape(shape)
    sin = jnp.sin(ang).reshape(shape)
    xf = x.astype(jnp.float32)
    x1, x2 = xf[..., :half], xf[..., half:]
    return jnp.concatenate([x1 * cos - x2 * sin, x1 * sin + x2 * cos], axis=-1).astype(x.dtype)


def _split_in(z):
    bounds = np.cumsum(np.array(IN_SIZES))[:-1].tolist()
    return jnp.split(z, bounds, axis=-1)


def _sgu_mixer(zu, zv, g_sgu, w_sp, b_sp):
    B, T, _ = zu.shape
    u = jax.nn.gelu(zu)
    v = _rmsnorm(jax.nn.gelu(zv), g_sgu)
    L = min(T, SGU_CHUNK)
    n_chunks = T // L
    w = jnp.where(jnp.tril(jnp.ones((L, L), dtype=bool))[None], w_sp[:, :L, :L], 0.0)
    vc = v.reshape(B, n_chunks, L, SGU_GROUPS, SGU_GROUP_CH)
    z = jnp.einsum('gts,bnsgc->bntgc', w, vc) + b_sp[:, :L].T[None, None, :, :, None]
    return u * z.reshape(B, T, SGU_WIDTH), v


def _mlstm_chunkwise(q, k, v, ig, lf, C0, n0, m0, chunk):
    B, T, H, DK = q.shape
    n_chunks = T // chunk

    def to_chunks(a):
        return jnp.moveaxis(a.reshape((B, n_chunks, chunk) + a.shape[2:]), 1, 0)

    causal = jnp.tril(jnp.ones((chunk, chunk), dtype=bool))[None, :, :, None]

    def step(carry, xs):
        C, n, m = carry
        qc, kc, vc, ic, fc = xs
        b = jnp.cumsum(fc, axis=1)
        logw = jnp.where(causal, b[:, :, None, :] - b[:, None, :, :] + ic[:, None, :, :], -jnp.inf)
        m_inter = m[:, None, :] + b
        m_t = jnp.maximum(m_inter, jnp.max(logw, axis=2))
        s = jnp.einsum('bthk,bshk->btsh', qc, kc) * jnp.exp(logw - m_t[:, :, None, :])
        a_inter = jnp.exp(m_inter - m_t)
        num = jnp.einsum('btsh,bshv->bthv', s, vc) + a_inter[..., None] * jnp.einsum('bthk,bhkv->bthv', qc, C)
        den = jnp.sum(s, axis=2) + a_inter * jnp.einsum('bthk,bhk->bth', qc, n)
        h = num / jnp.maximum(jnp.abs(den), jnp.exp(-m_t))[..., None]
        m_new = m_t[:, -1]
        b_last = b[:, -1]
        carry_decay = jnp.exp(m + b_last - m_new)
        w_write = jnp.exp(b_last[:, None, :] - b + ic - m_new[:, None, :])
        C_new = carry_decay[..., None, None] * C + jnp.einsum('bsh,bshk,bshv->bhkv', w_write, kc, vc)
        n_new = carry_decay[..., None] * n + jnp.einsum('bsh,bshk->bhk', w_write, kc)
        return (C_new, n_new, m_new), h

    xs = (to_chunks(q), to_chunks(k), to_chunks(v), to_chunks(ig), to_chunks(lf))
    (C, n, m), hs = lax.scan(step, (C0, n0, m0), xs)
    h = jnp.moveaxis(hs, 0, 1).reshape(B, T, H, v.shape[-1])
    return h, C, n, m


def _mlstm_mixer(zq, zk, zv, zo, zi, zf, g_h, C0, n0, m0, chunk):
    B, T, _ = zq.shape
    f32 = jnp.float32
    q = zq.reshape(B, T, MLSTM_HEADS, MLSTM_DK).astype(f32)
    k = zk.reshape(B, T, MLSTM_HEADS, MLSTM_DK).astype(f32) * (MLSTM_DK ** -0.5)
    v = zv.reshape(B, T, MLSTM_HEADS, MLSTM_DV).astype(f32)
    ig = zi.astype(f32)
    lf = jax.nn.log_sigmoid(zf.astype(f32))
    h, C, n, m = _mlstm_chunkwise(q, k, v, ig, lf, C0.astype(f32), n0.astype(f32), m0.astype(f32), chunk)
    h = _rmsnorm(h, g_h).reshape(B, T, MLSTM_HEADS * MLSTM_DV).astype(zo.dtype)
    return jax.nn.sigmoid(zo) * h, C.astype(zq.dtype), n.astype(zq.dtype), m.astype(zq.dtype)


def _mla_project(zcq, zckv, zkr, pos, P):
    cq = _rmsnorm(zcq, P['g_cq'])
    q = _rmsnorm(jnp.einsum('btr,rhd->bthd', cq, P['w_q_up']), P['g_qn'])
    q_nope, q_pe = q[..., :MLA_NOPE], _rope(q[..., MLA_NOPE:], pos)
    ckv = _rmsnorm(zckv, P['g_ckv'])
    k_nope = jnp.einsum('btr,rhn->bthn', ckv, P['w_uk'])
    kf = k_nope.astype(jnp.float32)
    krf = zkr.astype(jnp.float32)
    ms = (jnp.sum(kf * kf, axis=-1) + jnp.sum(krf * krf, axis=-1)[..., None]) / MLA_QK
    k_scale = lax.rsqrt(ms + EPS).astype(zkr.dtype)
    k_pe = _rope(zkr * P['g_kn'][MLA_NOPE:], pos)
    return q_nope, q_pe, ckv, k_nope, k_pe, k_scale


def _mla_prompt_attend(q_nope, q_pe, ckv, k_nope, k_pe, k_scale, P):
    f32 = jnp.float32
    B, T, H, _ = q_nope.shape
    s = k_scale.astype(f32)[..., None]
    k = jnp.concatenate([(k_nope * P['g_kn'][:MLA_NOPE]).astype(f32), jnp.broadcast_to(k_pe[:, :, None, :], (B, T, H, MLA_ROPE)).astype(f32)], axis=-1) * s
    q = jnp.concatenate([q_nope, q_pe], axis=-1).astype(f32) * (MLA_QK ** -0.5)
    v = jnp.einsum('btr,rhv->bthv', ckv, P['w_uv']).astype(f32)
    QB = min(T, MLA_Q_BLOCK)
    kpos = jnp.arange(T)

    def block(i):
        qb = lax.dynamic_slice_in_dim(q, i * QB, QB, axis=1)
        sc = jnp.einsum('bqhd,bkhd->bhqk', qb, k)
        qpos = i * QB + jnp.arange(QB)
        sc = jnp.where((kpos[None, :] <= qpos[:, None])[None, None], sc, -jnp.inf)
        return jnp.einsum('bhqk,bkhv->bqhv', jax.nn.softmax(sc, axis=-1), v)

    o = lax.map(block, jnp.arange(T // QB))
    return jnp.moveaxis(o, 0, 1).reshape(B, T, H * MLA_V).astype(ckv.dtype)


def _mla_sample_attend(q_nope, q_pe, ckv, k_nope, k_pe, k_scale, P, cache_ckv, cache_kpe, cache_kscale, page_table, layer):
    f32 = jnp.float32
    B, T, H, _ = q_nope.shape
    scale = MLA_QK ** -0.5
    q_lat = jnp.einsum('bthn,rhn->bthr', (q_nope * P['g_kn'][:MLA_NOPE]).astype(f32), P['w_uk'].astype(f32)) * scale
    qp = q_pe.astype(f32) * scale

    def scores(c, kp, ks):
        sc = jnp.einsum('bthr,bsr->bths', q_lat, c.astype(f32)) + jnp.einsum('bthe,bse->bths', qp, kp.astype(f32))
        return sc * jnp.swapaxes(ks.astype(f32), 1, 2)[:, None]

    causal = jnp.tril(jnp.ones((T, T), dtype=bool))[None, :, None, :]
    sc = jnp.where(causal, scores(ckv, k_pe, k_scale), -jnp.inf)
    m = jnp.max(sc, axis=-1)
    p = jnp.exp(sc - m[..., None])
    l = jnp.sum(p, axis=-1)
    acc = jnp.einsum('bths,bsr->bthr', p, ckv.astype(f32))

    def step(carry, pages):
        m, l, acc = carry
        c = cache_ckv[layer, pages]
        sc = scores(c, cache_kpe[layer, pages], cache_kscale[layer, pages])
        m_new = jnp.maximum(m, jnp.max(sc, axis=-1))
        alpha = jnp.exp(m - m_new)
        p = jnp.exp(sc - m_new[..., None])
        acc = alpha[..., None] * acc + jnp.einsum('bths,bsr->bthr', p, c.astype(f32))
        return (m_new, alpha * l + jnp.sum(p, axis=-1), acc), None

    (m, l, acc), _ = lax.scan(step, (m, l, acc), page_table.T)
    o = jnp.einsum('bthr,rhv->bthv', acc / l[..., None], P['w_uv'].astype(f32))
    return o.reshape(B, T, H * MLA_V).astype(ckv.dtype)


def _moe(h, w_router, b_router, w_eg, w_eu, w_ed):
    f32 = jnp.float32
    logits = jnp.einsum('btd,de->bte', h.astype(f32), w_router.astype(f32))
    scores = jax.nn.sigmoid(logits)
    B, T, _ = scores.shape
    sel = (scores + b_router.astype(f32)).reshape(B, T, N_EXPERT_GROUPS, EXPERTS_PER_GROUP)
    group_score = jnp.sum(lax.top_k(sel, 2)[0], axis=-1)
    g_idx = jnp.argmax(group_score, axis=-1)
    sel_in = jnp.einsum('btg,btge->bte', jax.nn.one_hot(g_idx, N_EXPERT_GROUPS, dtype=f32), sel)
    _, loc = lax.top_k(sel_in, TOP_K)
    eid = g_idx[..., None] * EXPERTS_PER_GROUP + loc
    w = jnp.take_along_axis(scores, eid, axis=-1)
    w = w / jnp.sum(w, axis=-1, keepdims=True)
    combine = jnp.einsum('btk,btke->bte', w, jax.nn.one_hot(eid, N_EXPERTS, dtype=f32)).astype(h.dtype)
    hg = jnp.einsum('btd,edf->btef', h, w_eg)
    hu = jnp.einsum('btd,edf->btef', h, w_eu)
    act = jax.nn.silu(hg) * hu * combine[..., None]
    return jnp.einsum('btef,efd->btd', act, w_ed)


def _trunk_layer(x, c, P, pos, C0, n0, m0, chunk, attend_fn, w_router, b_router):
    B, T, D = x.shape
    mod = jnp.einsum('bd,de->be', jax.nn.silu(c), P['w_ada']) + P['b_ada']
    sh1, sc1, g1, sh2, sc2, g2 = jnp.split(mod[:, None, :], 6, axis=-1)
    h = _rmsnorm(x, P['g_norm1']) * (1.0 + sc1) + sh1
    z = jnp.einsum('btd,dn->btn', h, P['w_in']) + P['b_in']
    zu, zv, zq, zk, zvv, zo, zi, zf, zcq, zckv, zkr = _split_in(z)
    out_a, v_sgu = _sgu_mixer(zu, zv, P['g_sgu'], P['w_spatial'], P['b_spatial'])
    out_b, C, n, m = _mlstm_mixer(zq, zk, zvv, zo, zi, zf, P['g_mlstm'], C0, n0, m0, chunk)
    q_nope, q_pe, ckv, k_nope, k_pe, k_scale = _mla_project(zcq, zckv, zkr, pos, P)
    out_c = attend_fn(q_nope, q_pe, ckv, k_nope, k_pe, k_scale, P)
    br = jnp.einsum('nbtc,ncd->btnd', jnp.stack([out_a, out_b, out_c]), P['w_br'])
    gates = jax.nn.sigmoid(jnp.einsum('btd,dg->btg', h, P['w_gate']) + P['b_gate']).reshape(B, T, N_BRANCH, D)
    mix = jnp.einsum('btd,de->bte', jnp.sum(gates * br, axis=2), P['w_o'])
    x = x + g1 * mix
    h2 = _rmsnorm(x, P['g_norm2']) * (1.0 + sc2) + sh2
    x = x + g2 * _moe(h2, w_router, b_router, P['w_exp_gate'], P['w_exp_up'], P['w_exp_down'])
    return x, (ckv, k_pe, k_scale, C, n, m, v_sgu)


def setup_inputs(seed: int = 0) -> dict:
    key = jax.random.key(seed)
    ks = iter(jax.random.split(key, 48))
    f32 = jnp.float32

    def nrm(shape, scale):
        return jax.random.normal(next(ks), shape, f32) * scale

    def gain(shape):
        return 1.0 + nrm(shape, 0.05)

    n_pages = PAST_LEN // PAGE_SIZE
    n_phys = (DEC_BATCH * n_pages * 5) // 4
    D = D_MODEL
    inp = {}
    inp['x_prompt'] = nrm((BATCH, SEQ, D), 1.0)
    inp['x_sample'] = nrm((DEC_BATCH, DEC_SEQ, D), 1.0)
    inp['c_prompt'] = nrm((BATCH, D), 1.0)
    inp['c_sample'] = nrm((DEC_BATCH, D), 1.0)
    inp['cache_ckv'] = nrm((DEPTH, n_phys, PAGE_SIZE, MLA_KV_RANK), 1.0)
    inp['cache_kpe'] = nrm((DEPTH, n_phys, PAGE_SIZE, MLA_ROPE), 1.0)
    inp['cache_kscale'] = jax.random.uniform(next(ks), (DEPTH, n_phys, PAGE_SIZE, MLA_HEADS), f32, 0.5, 1.5)
    inp['page_table'] = jax.random.permutation(next(ks), n_phys)[:DEC_BATCH * n_pages].reshape(DEC_BATCH, n_pages).astype(jnp.int32)
    inp['state_C'] = nrm((DEPTH, DEC_BATCH, MLSTM_HEADS, MLSTM_DK, MLSTM_DV), 0.1)
    inp['state_n'] = nrm((DEPTH, DEC_BATCH, MLSTM_HEADS, MLSTM_DK), 0.1)
    inp['state_m'] = nrm((DEPTH, DEC_BATCH, MLSTM_HEADS), 1.0)
    inp['w_ada'] = nrm((DEPTH, D, 6 * D), 0.5 * D ** -0.5)
    inp['b_ada'] = nrm((DEPTH, 6 * D), 0.02)
    inp['g_norm1'] = gain((DEPTH, D))
    inp['g_norm2'] = gain((DEPTH, D))
    inp['w_in'] = nrm((DEPTH, D, N_IN), D ** -0.5)
    inp['b_in'] = nrm((DEPTH, N_IN), 0.02).at[:, FORGET_COL:FORGET_COL + MLSTM_HEADS].add(FORGET_BIAS)
    inp['g_sgu'] = gain((DEPTH, SGU_WIDTH))
    inp['w_spatial'] = nrm((DEPTH, SGU_GROUPS, SGU_CHUNK, SGU_CHUNK), SGU_CHUNK ** -0.5)
    inp['b_spatial'] = gain((DEPTH, SGU_GROUPS, SGU_CHUNK))
    inp['g_mlstm'] = gain((DEPTH, MLSTM_HEADS, MLSTM_DV))
    inp['g_cq'] = gain((DEPTH, MLA_Q_RANK))
    inp['w_q_up'] = nrm((DEPTH, MLA_Q_RANK, MLA_HEADS, MLA_QK), MLA_Q_RANK ** -0.5)
    inp['g_ckv'] = gain((DEPTH, MLA_KV_RANK))
    inp['w_uk'] = nrm((DEPTH, MLA_KV_RANK, MLA_HEADS, MLA_NOPE), MLA_KV_RANK ** -0.5)
    inp['w_uv'] = nrm((DEPTH, MLA_KV_RANK, MLA_HEADS, MLA_V), MLA_KV_RANK ** -0.5)
    inp['g_qn'] = gain((DEPTH, MLA_QK))
    inp['g_kn'] = gain((DEPTH, MLA_QK))
    inp['w_br'] = nrm((DEPTH, N_BRANCH, BRANCH_WIDTH, D), BRANCH_WIDTH ** -0.5)
    inp['w_gate'] = nrm((DEPTH, D, N_BRANCH * D), D ** -0.5)
    inp['b_gate'] = nrm((DEPTH, N_BRANCH * D), 0.02)
    inp['w_o'] = nrm((DEPTH, D, D), D ** -0.5)
    inp['w_router'] = nrm((D, N_EXPERTS), D ** -0.5)
    inp['b_router'] = nrm((N_EXPERTS,), 0.01)
    inp['w_exp_gate'] = nrm((DEPTH, N_EXPERTS, D, D_EXPERT), D ** -0.5)
    inp['w_exp_up'] = nrm((DEPTH, N_EXPERTS, D, D_EXPERT), D ** -0.5)
    inp['w_exp_down'] = nrm((DEPTH, N_EXPERTS, D_EXPERT, D), D_EXPERT ** -0.5)
    return inp


def reference(x_prompt, x_sample, c_prompt, c_sample, cache_ckv, cache_kpe, cache_kscale, page_table, state_C, state_n, state_m, w_ada, b_ada, g_norm1, g_norm2, w_in, b_in, g_sgu, w_spatial, b_spatial, g_mlstm, g_cq, w_q_up, g_ckv, w_uk, w_uv, g_qn, g_kn, w_br, w_gate, b_gate, w_o, w_router, b_router, w_exp_gate, w_exp_up, w_exp_down):
    f32 = jnp.float32
    B_p, T_p, _ = x_prompt.shape
    T_s = x_sample.shape[1]
    past_len = page_table.shape[1] * PAGE_SIZE
    pos_p = jnp.arange(T_p, dtype=f32)
    pos_s = jnp.arange(T_s, dtype=f32) + past_len
    C0 = jnp.zeros((B_p, MLSTM_HEADS, MLSTM_DK, MLSTM_DV), f32)
    n0 = jnp.zeros((B_p, MLSTM_HEADS, MLSTM_DK), f32)
    m0 = jnp.zeros((B_p, MLSTM_HEADS), f32)
    xp, xs = x_prompt, x_sample
    st_p, st_s = [], []
    for l in range(DEPTH):
        P = {'w_ada': w_ada[l], 'b_ada': b_ada[l], 'g_norm1': g_norm1[l], 'g_norm2': g_norm2[l],
             'w_in': w_in[l], 'b_in': b_in[l], 'g_sgu': g_sgu[l], 'w_spatial': w_spatial[l], 'b_spatial': b_spatial[l],
             'g_mlstm': g_mlstm[l], 'g_cq': g_cq[l], 'w_q_up': w_q_up[l], 'g_ckv': g_ckv[l], 'w_uk': w_uk[l],
             'w_uv': w_uv[l], 'g_qn': g_qn[l], 'g_kn': g_kn[l], 'w_br': w_br[l], 'w_gate': w_gate[l],
             'b_gate': b_gate[l], 'w_o': w_o[l], 'w_exp_gate': w_exp_gate[l], 'w_exp_up': w_exp_up[l],
             'w_exp_down': w_exp_down[l]}
        xp, sp = _trunk_layer(xp, c_prompt, P, pos_p, C0, n0, m0, min(T_p, MLSTM_CHUNK), _mla_prompt_attend, w_router, b_router)
        attend_s = functools.partial(_mla_sample_attend, cache_ckv=cache_ckv, cache_kpe=cache_kpe, cache_kscale=cache_kscale, page_table=page_table, layer=l)
        xs, ss = _trunk_layer(xs, c_sample, P, pos_s, state_C[l], state_n[l], state_m[l], T_s, attend_s, w_router, b_router)
        st_p.append(sp)
        st_s.append(ss)
    ckv_p = jnp.stack([s[0] for s in st_p])
    kpe_p = jnp.stack([s[1] for s in st_p])
    kscale_p = jnp.stack([s[2] for s in st_p])
    C_p = jnp.stack([s[3] for s in st_p])
    n_p = jnp.stack([s[4] for s in st_p])
    m_p = jnp.stack([s[5] for s in st_p])
    ckv_s = jnp.stack([s[0] for s in st_s])
    kpe_s = jnp.stack([s[1] for s in st_s])
    kscale_s = jnp.stack([s[2] for s in st_s])
    C_s = jnp.stack([s[3] for s in st_s])
    n_s = jnp.stack([s[4] for s in st_s])
    m_s = jnp.stack([s[5] for s in st_s])
    sgu_v_s = jnp.stack([s[6] for s in st_s])
    return (xp, xs, ckv_p, kpe_p, kscale_p, C_p, n_p, m_p, ckv_s, kpe_s, kscale_s, C_s, n_s, m_s, sgu_v_s)
```

```python
import functools
import math

import jax
import jax.numpy as jnp
from jax import lax
from jax.experimental import pallas as pl
from jax.experimental.pallas import tpu as pltpu

F32 = jnp.float32
BF16 = jnp.bfloat16
NEG_INF = float("-inf")

EPS = 1e-6
D = 1024
SGU_G = 4
SGU_C = 128
SGU_W = SGU_G * SGU_C
LH = 4
LDK = 128
LDV = 128
LCHUNK = 128
AH = 8
A_QR = 256
A_KVR = 256
A_NOPE = 64
A_ROPE = 32
A_QK = A_NOPE + A_ROPE
A_V = 64
ROPE_BASE = 10000.0
PAGE = 128
N_EXP = 16
N_GRP = 4
GRP_E = 4
D_EXP = 256

LANE = 128
C_ZU, C_ZV, C_ZQ, C_ZK, C_ZVV, C_ZO = 0, 512, 1024, 1536, 2048, 2560
C_ZC = 3072
C_SMALL = 3584
C_GATE = 3712
N_CAT = C_GATE + 3 * D
SM_KR, SM_I, SM_F = 64, 96, 100

VMEM_LIMIT = 56 * 1024 * 1024


def _cparams(sem):
    return pltpu.CompilerParams(dimension_semantics=sem, vmem_limit_bytes=VMEM_LIMIT)


def _const_spec(shape):
    nd = len(shape)
    return pl.BlockSpec(shape, lambda *_: (0,) * nd, pipeline_mode=pl.Buffered(1))


def _sigmoid(x):
    return 0.5 * jnp.tanh(0.5 * x) + 0.5


def _dot(a, b):
    return jnp.dot(a, b, preferred_element_type=F32)


def _dot_nt(a, b):
    return lax.dot_general(a, b, (((1,), (1,)), ((), ())), preferred_element_type=F32)


def _rms(x, g):
    return x * lax.rsqrt(jnp.mean(x * x, axis=-1, keepdims=True) + EPS) * g


def _ada_kernel(c_ref, w_ref, b_ref, o_ref):
    c = c_ref[...]
    s = (c * _sigmoid(c)).astype(BF16)
    o_ref[...] = _dot(s, w_ref[...].astype(BF16)) + b_ref[...]


def _ada(c_all, w_ada, b_ada):
    depth, _, n = w_ada.shape
    bc = c_all.shape[0]
    tn = 1536
    return pl.pallas_call(
        _ada_kernel,
        out_shape=jax.ShapeDtypeStruct((depth, bc, n), F32),
        grid=(depth, n // tn),
        in_specs=[
            pl.BlockSpec((bc, D), lambda l, j: (0, 0)),
            pl.BlockSpec((None, D, tn), lambda l, j: (l, 0, j)),
            pl.BlockSpec((None, 1, tn), lambda l, j: (l, 0, j)),
        ],
        out_specs=pl.BlockSpec((None, bc, tn), lambda l, j: (l, 0, j)),
        compiler_params=_cparams(("arbitrary", "arbitrary")),
        name="ada",
    )(c_all, w_ada, b_ada.reshape(depth, 1, n))


def _proj_kernel(x_ref, sc_ref, sh_ref, w_ref, b_ref, g1_ref, gsgu_ref, wmix_ref, bmix_ref,
                 oa_ref, q_ref, k_ref, v_ref, og_ref, zc_ref, small_ref, gates_ref, *rest, mix_rows, emit_v):
    tm = x_ref.shape[0]
    x = x_ref[...]
    h = _rms(x, g1_ref[...]) * (1.0 + sc_ref[...]) + sh_ref[...]
    hb = h.astype(BF16)

    def sec(a, b):
        return _dot(hb, w_ref[:, a:b]) + b_ref[:, a:b]

    u = jax.nn.gelu(sec(C_ZU, C_ZU + SGU_W))
    vv = _rms(jax.nn.gelu(sec(C_ZV, C_ZV + SGU_W)), gsgu_ref[...])
    if emit_v:
        rest[0][...] = vv
    vb = vv.astype(BF16)
    for g in range(SGU_G):
        cs = slice(g * SGU_C, (g + 1) * SGU_C)
        for c in range(tm // mix_rows):
            rs = slice(c * mix_rows, (c + 1) * mix_rows)
            z = _dot(wmix_ref[g], vb[rs, cs]) + bmix_ref[:, cs]
            oa_ref[rs, cs] = (u[rs, cs] * z).astype(oa_ref.dtype)

    q_ref[...] = sec(C_ZQ, C_ZQ + 512).astype(q_ref.dtype)
    k_ref[...] = (sec(C_ZK, C_ZK + 512) * (LDK ** -0.5)).astype(k_ref.dtype)
    v_ref[...] = sec(C_ZVV, C_ZVV + 512).astype(v_ref.dtype)
    og_ref[...] = _sigmoid(sec(C_ZO, C_ZO + 512)).astype(og_ref.dtype)
    zc_ref[...] = sec(C_ZC, C_ZC + 512)
    small_ref[...] = sec(C_SMALL, C_SMALL + LANE)
    for n in range(3):
        a = C_GATE + n * D
        gates_ref[:, n * D:(n + 1) * D] = _sigmoid(sec(a, a + D)).astype(gates_ref.dtype)


def _proj(x2, sc, sh, wcat, bcat, g1, gsgu, wmix, bmix, *, tm, rows_per_mod, emit_v):
    m = x2.shape[0]
    mix_rows = wmix.shape[-1]
    if rows_per_mod is None:
        mod_spec = pl.BlockSpec((tm, D), lambda i: (i, 0))
    else:
        tpb = rows_per_mod // tm
        mod_spec = pl.BlockSpec((None, 1, D), lambda i: (i // tpb, 0, 0))
    row = lambda w: pl.BlockSpec((tm, w), lambda i: (i, 0))
    out_shape = [
        jax.ShapeDtypeStruct((m, 512), BF16),
        jax.ShapeDtypeStruct((m, 512), BF16),
        jax.ShapeDtypeStruct((m, 512), BF16),
        jax.ShapeDtypeStruct((m, 512), BF16),
        jax.ShapeDtypeStruct((m, 512), BF16),
        jax.ShapeDtypeStruct((m, 512), F32),
        jax.ShapeDtypeStruct((m, LANE), F32),
        jax.ShapeDtypeStruct((m, 3 * D), BF16),
    ]
    out_specs = [row(512)] * 5 + [row(512), row(LANE), row(3 * D)]
    if emit_v:
        out_shape.append(jax.ShapeDtypeStruct((m, SGU_W), F32))
        out_specs.append(row(SGU_W))
    return pl.pallas_call(
        functools.partial(_proj_kernel, mix_rows=mix_rows, emit_v=emit_v),
        out_shape=out_shape,
        grid=(m // tm,),
        in_specs=[
            row(D), mod_spec, mod_spec,
            _const_spec(wcat.shape), _const_spec(bcat.shape), _const_spec(g1.shape), _const_spec(gsgu.shape),
            _const_spec(wmix.shape), _const_spec(bmix.shape),
        ],
        out_specs=out_specs,
        compiler_params=_cparams(("arbitrary",)),
        name="proj",
    )(x2, sc, sh, wcat, bcat, g1, gsgu, wmix, bmix)


def _mla_kernel(zc_ref, small_ref, cos_ref, sin_ref, wq_ref, wqr_ref, gq_ref, gqr_ref, wuk_ref, wuv_ref, vone_ref,
                gcq_ref, gckv_ref, gkn_ref, gkp_ref, wukt_ref,
                ckv_ref, kpe_ref, ksc_ref, *outs, sample):
    tm = zc_ref.shape[0]
    cosv = cos_ref[...]
    sinv = sin_ref[...]
    lane = lax.broadcasted_iota(jnp.int32, (tm, LANE), 1)
    scale = A_QK ** -0.5

    cq = _rms(zc_ref[:, 0:A_QR], gcq_ref[...]).astype(BF16)
    q = _dot(cq, wq_ref[...])
    qr = _dot(cq, wqr_ref[...])

    ckv = _rms(zc_ref[:, A_QR:A_QR + A_KVR], gckv_ref[...])
    ckv_ref[...] = ckv
    ckvb = ckv.astype(BF16)
    kn = _dot(ckvb, wuk_ref[...])

    small = small_ref[...]
    zkr = jnp.where((lane >= SM_KR) & (lane < SM_KR + A_ROPE), small, 0.0)
    t = zkr * gkp_ref[...]
    half = A_ROPE // 2
    rot = jnp.where(lane < SM_KR + half, pltpu.roll(t, LANE - half, 1), pltpu.roll(t, half, 1))
    kpe = t * cosv + rot * sinv
    kpe_ref[...] = kpe[:, SM_KR:SM_KR + A_ROPE]

    ksc = jnp.zeros((tm, LANE), F32)
    if sample:
        qa_ref, ql_ref = outs
    else:
        qa_ref, ka_ref, va_ref = outs
        va_ref[...] = (_dot(ckvb, wuv_ref[...]) + vone_ref[...]).astype(va_ref.dtype)
    for h in range(AH):
        hs = slice(h * LANE, (h + 1) * LANE)
        qb = q[:, hs]
        rinv = lax.rsqrt(jnp.sum(qb * qb, axis=-1, keepdims=True) / A_QK + EPS)
        qh = (qb * rinv * gq_ref[:, hs] * cosv + qr[:, hs] * rinv * gqr_ref[:, hs] * sinv)
        kraw = kn[:, hs] + zkr
        ks = lax.rsqrt(jnp.sum(kraw * kraw, axis=-1, keepdims=True) / A_QK + EPS)
        ksc = jnp.where(lane == h, ks, ksc)
        if sample:
            qa_ref[:, hs] = qh * scale
            qn = (qh * gkn_ref[...]).astype(BF16)
            ql_ref[:, h * A_KVR:(h + 1) * A_KVR] = _dot(qn, wukt_ref[h]) * scale
        else:
            qa_ref[:, hs] = (qh * scale).astype(qa_ref.dtype)
            ka_ref[:, hs] = ((kn[:, hs] * gkn_ref[...] + kpe) * ks).astype(ka_ref.dtype)
    ksc_ref[...] = ksc[:, 0:AH]


def _mla_prep(zc, small, cos_t, sin_t, lw, *, tm, rows_per_pos, sample):
    m = zc.shape[0]
    if rows_per_pos is None:
        pos_spec = pl.BlockSpec((tm, LANE), lambda i: (i, 0))
    else:
        tpb = rows_per_pos // tm
        pos_spec = pl.BlockSpec((tm, LANE), lambda i: (i % tpb, 0))
    row = lambda w: pl.BlockSpec((tm, w), lambda i: (i, 0))
    out_shape = [
        jax.ShapeDtypeStruct((m, A_KVR), F32),
        jax.ShapeDtypeStruct((m, A_ROPE), F32),
        jax.ShapeDtypeStruct((m, AH), F32),
    ]
    out_specs = [row(A_KVR), row(A_ROPE), row(AH)]
    if sample:
        out_shape += [jax.ShapeDtypeStruct((m, AH * LANE), F32), jax.ShapeDtypeStruct((m, AH * A_KVR), F32)]
        out_specs += [row(AH * LANE), row(AH * A_KVR)]
    else:
        out_shape += [jax.ShapeDtypeStruct((m, AH * LANE), BF16)] * 3
        out_specs += [row(AH * LANE)] * 3
    names = ("wq", "wqr", "gq", "gqr", "wuk", "wuv", "vone", "gcq", "gckv", "gkn", "gkp", "wukt")
    ws = [lw[n] for n in names]
    return pl.pallas_call(
        functools.partial(_mla_kernel, sample=sample),
        out_shape=out_shape,
        grid=(m // tm,),
        in_specs=[row(512), row(LANE), pos_spec, pos_spec] + [_const_spec(w.shape) for w in ws],
        out_specs=out_specs,
        compiler_params=_cparams(("arbitrary",)),
        name="mla_prep",
    )(zc, small, cos_t, sin_t, *ws)


def _mlstm_kernel(q_ref, k_ref, v_ref, og_ref, small_ref, gh_ref, c0_ref, n0_ref, m0_ref,
                  h_ref, c_ref, n_ref, m_ref):
    L = q_ref.shape[0]

    @pl.when(pl.program_id(1) == 0)
    def _():
        c_ref[...] = c0_ref[...]
        n_ref[...] = n0_ref[...]
        m_ref[...] = m0_ref[...]

    small = small_ref[...]
    row = lax.broadcasted_iota(jnp.int32, (L, L), 0)
    col = lax.broadcasted_iota(jnp.int32, (L, L), 1)
    causal = col <= row
    eye = col == row
    lane = lax.broadcasted_iota(jnp.int32, (1, LANE), 1)
    m_all = m_ref[...]
    m_out = m_all
    for h in range(LH):
        hs = slice(h * LDK, (h + 1) * LDK)
        i_col = small[:, SM_I + h:SM_I + h + 1]
        zf = small[:, SM_F + h:SM_F + h + 1]
        lf_col = jnp.minimum(zf, 0.0) - jnp.log1p(jnp.exp(-jnp.abs(zf)))
        b_row = jnp.sum(jnp.where(row <= col, lf_col, 0.0), axis=0, keepdims=True)
        b_col = jnp.sum(jnp.where(eye, b_row, 0.0), axis=1, keepdims=True)
        i_row = jnp.sum(jnp.where(eye, i_col, 0.0), axis=0, keepdims=True)
        logw = jnp.where(causal, b_col - b_row + i_row, NEG_INF)
        m_prev = m_all[:, h:h + 1]
        m_inter = m_prev + b_col
        m_t = jnp.maximum(m_inter, jnp.max(logw, axis=1, keepdims=True))
        qh = q_ref[:, hs]
        kh = k_ref[:, hs]
        vh = v_ref[:, hs]
        smat = _dot_nt(qh, kh) * jnp.exp(logw - m_t)
        a_inter = jnp.exp(m_inter - m_t)
        c_h = c_ref[h]
        n_h = n_ref[h:h + 1, :]
        num = _dot(smat.astype(BF16), vh) + a_inter * _dot(qh, c_h.astype(BF16))
        den = jnp.sum(smat, axis=1, keepdims=True) + a_inter * jnp.sum(qh.astype(F32) * n_h, axis=1, keepdims=True)
        hv = num / jnp.maximum(jnp.abs(den), jnp.exp(-m_t))
        hn = _rms(hv, gh_ref[h:h + 1, :])
        h_ref[:, hs] = (og_ref[:, hs].astype(F32) * hn).astype(h_ref.dtype)
        m_new = m_t[L - 1:L, :]
        b_last = b_col[L - 1:L, :]
        decay = jnp.exp(m_prev + b_last - m_new)
        w_col = jnp.exp(b_last - b_col + i_col - m_new)
        wk = kh.astype(F32) * w_col
        c_ref[h] = decay * c_h + _dot(wk.T.astype(BF16), vh)
        n_ref[h:h + 1, :] = decay * n_h + jnp.sum(wk, axis=0, keepdims=True)
        m_out = jnp.where(lane == h, m_new, m_out)
    m_ref[...] = m_out


def _mlstm(q, k, v, og, small, gh, c0, n0, m0, *, n_chunks):
    m = q.shape[0]
    b = c0.shape[0]
    L = m // (b * n_chunks)
    row = lambda w: pl.BlockSpec((L, w), lambda i, c: (i * n_chunks + c, 0))
    st_c = pl.BlockSpec((None, LH, LDK, LDV), lambda i, c: (i, 0, 0, 0))
    st_n = pl.BlockSpec((None, LH, LDK), lambda i, c: (i, 0, 0))
    st_m = pl.BlockSpec((None, 1, LANE), lambda i, c: (i, 0, 0))
    return pl.pallas_call(
        _mlstm_kernel,
        out_shape=[
            jax.ShapeDtypeStruct((m, 512), BF16),
            jax.ShapeDtypeStruct((b, LH, LDK, LDV), F32),
            jax.ShapeDtypeStruct((b, LH, LDK), F32),
            jax.ShapeDtypeStruct((b, 1, LANE), F32),
        ],
        grid=(b, n_chunks),
        in_specs=[row(512), row(512), row(512), row(512), row(LANE), _const_spec(gh.shape), st_c, st_n, st_m],
        out_specs=[row(512), st_c, st_n, st_m],
        compiler_params=_cparams(("arbitrary", "arbitrary")),
        name="mlstm",
    )(q, k, v, og, small, gh, c0, n0, m0)


def _attn_kernel(q_ref, k_ref, v_ref, o_ref, *, tk):
    tq = q_ref.shape[0]
    i = pl.program_id(1)
    n_diag = tq // tk
    n_full = i * n_diag
    rows = i * tq + lax.broadcasted_iota(jnp.int32, (tq, tk), 0)
    cols0 = lax.broadcasted_iota(jnp.int32, (tq, tk), 1)
    for h in range(AH):
        hs = slice(h * LANE, (h + 1) * LANE)
        qh = q_ref[:, hs]

        def step(j, carry, masked):
            m_i, acc = carry
            start = pl.multiple_of(j * tk, tk)
            s = _dot_nt(qh, k_ref[pl.ds(start, tk), hs])
            if masked:
                s = jnp.where(cols0 + start <= rows, s, NEG_INF)
            m_new = jnp.maximum(m_i, jnp.max(s, axis=-1, keepdims=True))
            p = jnp.exp(s - m_new)
            acc = jnp.exp(m_i - m_new) * acc + _dot(p.astype(BF16), v_ref[pl.ds(start, tk), hs])
            return m_new, acc

        carry = (jnp.full((tq, 1), NEG_INF, F32), jnp.zeros((tq, LANE), F32))
        carry = lax.fori_loop(0, n_full, functools.partial(step, masked=False), carry)
        for jj in range(n_diag):
            carry = step(n_full + jj, carry, True)
        acc = carry[1]
        o_ref[:, hs] = (acc / acc[:, A_V:A_V + 1]).astype(o_ref.dtype)


def _attn_prompt(qa, ka, va, *, b, t, tq, tk):
    nq = t // tq
    return pl.pallas_call(
        functools.partial(_attn_kernel, tk=tk),
        out_shape=jax.ShapeDtypeStruct(qa.shape, BF16),
        grid=(b, nq),
        in_specs=[
            pl.BlockSpec((tq, AH * LANE), lambda i, j: (i * nq + j, 0)),
            pl.BlockSpec((t, AH * LANE), lambda i, j: (i, 0)),
            pl.BlockSpec((t, AH * LANE), lambda i, j: (i, 0)),
        ],
        out_specs=pl.BlockSpec((tq, AH * LANE), lambda i, j: (i * nq + j, 0)),
        compiler_params=_cparams(("arbitrary", "arbitrary")),
        name="attn_prompt",
    )(qa, ka, va)


def _paged_kernel(pt_ref, ql_ref, qp_ref, cn_ref, kpn_ref, ksn_ref, *rest, n_per_step, t_new):
    del pt_ref
    pages = rest[:3 * n_per_step]
    o_ref, m_sc, l_sc, acc_sc = rest[3 * n_per_step:]
    j = pl.program_id(1)
    ql = ql_ref[...]
    qp = qp_ref[...]
    nr = ql.shape[0]

    @pl.when(j == 0)
    def _():
        cn = cn_ref[...].astype(BF16)
        s = (_dot_nt(ql, cn) + _dot(qp, kpn_ref[...].astype(BF16))) * ksn_ref[...]
        nk = cn.shape[0]
        r_t = lax.broadcasted_iota(jnp.int32, (nr, nk), 0) // AH
        key = lax.broadcasted_iota(jnp.int32, (nr, nk), 1)
        s = jnp.where((key <= r_t) & (key < t_new), s, NEG_INF)
        m0 = jnp.max(s, axis=-1, keepdims=True)
        p = jnp.exp(s - m0)
        m_sc[...] = m0
        l_sc[...] = jnp.sum(p, axis=-1, keepdims=True)
        acc_sc[...] = _dot(p.astype(BF16), cn)

    m_i = m_sc[...]
    l_i = l_sc[...]
    acc = acc_sc[...]
    for n in range(n_per_step):
        cb = pages[3 * n][...].astype(BF16)
        kpt = pages[3 * n + 1][...].astype(BF16)
        ks = pages[3 * n + 2][...]
        kst = jnp.concatenate([ks] * t_new, axis=0)
        s = (_dot_nt(ql, cb) + _dot(qp, kpt)) * kst
        m_new = jnp.maximum(m_i, jnp.max(s, axis=-1, keepdims=True))
        alpha = jnp.exp(m_i - m_new)
        p = jnp.exp(s - m_new)
        l_i = alpha * l_i + jnp.sum(p, axis=-1, keepdims=True)
        acc = alpha * acc + _dot(p.astype(BF16), cb)
        m_i = m_new
    m_sc[...] = m_i
    l_sc[...] = l_i
    acc_sc[...] = acc

    @pl.when(j == pl.num_programs(1) - 1)
    def _():
        o_ref[...] = acc / l_i


def _attn_paged(page_table, qlat, qp, c_new, kp_new_t, ks_new_t, cache_ckv, cache_kpe_t, cache_ks_t, *, layer, t_new,
                n_per_step):
    bs, nr, _ = qlat.shape
    n_pages = page_table.shape[1]
    nk = c_new.shape[1]
    steps = n_pages // n_per_step
    pt_flat = page_table.reshape(-1)

    def page_spec(rows, width, n):
        return pl.BlockSpec((None, None, rows, width),
                            lambda b, j, pt: (layer, pt[b * n_pages + j * n_per_step + n], 0, 0))

    page_specs, page_args = [], []
    for n in range(n_per_step):
        page_specs += [page_spec(PAGE, A_KVR, n), page_spec(A_ROPE, PAGE, n), page_spec(AH, PAGE, n)]
        page_args += [cache_ckv, cache_kpe_t, cache_ks_t]
    per_b = lambda s1, s2: pl.BlockSpec((None, s1, s2), lambda b, j, pt: (b, 0, 0))
    grid_spec = pltpu.PrefetchScalarGridSpec(
        num_scalar_prefetch=1,
        grid=(bs, steps),
        in_specs=[per_b(nr, A_KVR), per_b(nr, A_ROPE), per_b(nk, A_KVR), per_b(A_ROPE, nk), per_b(nr, nk)] + page_specs,
        out_specs=per_b(nr, A_KVR),
        scratch_shapes=[pltpu.VMEM((nr, 1), F32), pltpu.VMEM((nr, 1), F32), pltpu.VMEM((nr, A_KVR), F32)],
    )
    return pl.pallas_call(
        functools.partial(_paged_kernel, n_per_step=n_per_step, t_new=t_new),
        out_shape=jax.ShapeDtypeStruct((bs, nr, A_KVR), F32),
        grid_spec=grid_spec,
        compiler_params=_cparams(("arbitrary", "arbitrary")),
        name="attn_paged",
    )(pt_flat, qlat, qp, c_new, kp_new_t, ks_new_t, *page_args)


def _uv_kernel(x_ref, w_ref, o_ref):
    o_ref[...] = _dot(x_ref[...].astype(BF16), w_ref[...]).astype(o_ref.dtype)


def _uv_proj(olat_h, wuvp):
    _, m, _ = olat_h.shape
    return pl.pallas_call(
        _uv_kernel,
        out_shape=jax.ShapeDtypeStruct((m, AH * LANE), BF16),
        grid=(AH,),
        in_specs=[pl.BlockSpec((None, m, A_KVR), lambda h: (h, 0, 0)),
                  pl.BlockSpec((None, A_KVR, LANE), lambda h: (h, 0, 0))],
        out_specs=pl.BlockSpec((m, LANE), lambda h: (0, h)),
        compiler_params=_cparams(("arbitrary",)),
        name="uv_proj",
    )(olat_h, wuvp)


def _merge_kernel(x_ref, g1_ref, oa_ref, ob_ref, oc_ref, gates_ref, wa_ref, wb_ref, wc_ref, wo_ref, o_ref):
    mix = gates_ref[:, 0:D].astype(F32) * _dot(oa_ref[...], wa_ref[...])
    mix += gates_ref[:, D:2 * D].astype(F32) * _dot(ob_ref[...], wb_ref[...])
    mix += gates_ref[:, 2 * D:3 * D].astype(F32) * _dot(oc_ref[...], wc_ref[...])
    o_ref[...] = x_ref[...] + g1_ref[...] * _dot(mix.astype(BF16), wo_ref[...])


def _mod_spec(tm, rows_per_mod):
    if rows_per_mod is None:
        return pl.BlockSpec((tm, D), lambda i: (i, 0))
    tpb = rows_per_mod // tm
    return pl.BlockSpec((None, 1, D), lambda i: (i // tpb, 0, 0))


def _merge(x2, g1, oa, ob, oc, gates, wa, wb, wc, wo, *, tm, rows_per_mod):
    m = x2.shape[0]
    row = lambda w: pl.BlockSpec((tm, w), lambda i: (i, 0))
    return pl.pallas_call(
        _merge_kernel,
        out_shape=jax.ShapeDtypeStruct((m, D), F32),
        grid=(m // tm,),
        in_specs=[row(D), _mod_spec(tm, rows_per_mod), row(512), row(512), row(AH * LANE), row(3 * D),
                  _const_spec(wa.shape), _const_spec(wb.shape), _const_spec(wc.shape), _const_spec(wo.shape)],
        out_specs=row(D),
        compiler_params=_cparams(("arbitrary",)),
        name="merge",
    )(x2, g1, oa, ob, oc, gates, wa, wb, wc, wo)


def _moe_kernel(x_ref, sc_ref, sh_ref, g2_ref, gn_ref, wr1_ref, wr2_ref, br_ref, wg_ref, wu_ref, wd_ref, o_ref):
    tm = x_ref.shape[0]
    x = x_ref[...]
    h2 = _rms(x, gn_ref[...]) * (1.0 + sc_ref[...]) + sh_ref[...]
    hi = h2.astype(BF16)
    lo = (h2 - hi.astype(F32)).astype(BF16)
    r = _dot(hi, wr1_ref[...]) + _dot(lo, wr2_ref[...])
    rt = r.T
    logits = rt[0:32] + rt[32:64]
    scores = _sigmoid(logits)
    sel = scores + br_ref[...]
    e = [sel[8 * p:8 * p + 8] for p in range(GRP_E)]
    sc = [scores[8 * p:8 * p + 8] for p in range(GRP_E)]
    a, b = jnp.maximum(e[0], e[1]), jnp.minimum(e[0], e[1])
    c, d = jnp.maximum(e[2], e[3]), jnp.minimum(e[2], e[3])
    top1 = jnp.maximum(a, c)
    top2 = jnp.maximum(jnp.minimum(a, c), jnp.maximum(b, d))
    rowi = lax.broadcasted_iota(jnp.int32, (8, tm), 0).astype(F32)
    gs = jnp.where(rowi < N_GRP, top1 + top2, NEG_INF)
    gmax = jnp.max(gs, axis=0, keepdims=True)
    gidx = jnp.min(jnp.where(gs == gmax, rowi, 8.0), axis=0, keepdims=True)
    oh = rowi == gidx
    s = [jnp.sum(jnp.where(oh, e[p], 0.0), axis=0, keepdims=True) for p in range(GRP_E)]
    w = [jnp.sum(jnp.where(oh, sc[p], 0.0), axis=0, keepdims=True) for p in range(GRP_E)]

    def first_argmax(vals):
        best = jnp.maximum(jnp.maximum(vals[0], vals[1]), jnp.maximum(vals[2], vals[3]))
        return jnp.where(vals[0] == best, 0, jnp.where(vals[1] == best, 1, jnp.where(vals[2] == best, 2, 3)))

    i1 = first_argmax(s)
    i2 = first_argmax([jnp.where(i1 == p, NEG_INF, s[p]) for p in range(GRP_E)])
    w1 = sum(jnp.where(i1 == p, w[p], 0.0) for p in range(GRP_E))
    w2 = sum(jnp.where(i2 == p, w[p], 0.0) for p in range(GRP_E))
    den = w1 + w2
    w1 = w1 / den
    w2 = w2 / den
    comb = [jnp.where(oh, jnp.where(i1 == p, w1, 0.0) + jnp.where(i2 == p, w2, 0.0), 0.0) for p in range(GRP_E)]
    comb_t = jnp.concatenate(comb + [jnp.zeros((LANE - 8 * GRP_E, tm), F32)], axis=0)
    combine = comb_t.T

    acc = jnp.zeros((tm, D), F32)
    for g in range(N_GRP):
        acts = []
        for p in range(GRP_E):
            ex = g * GRP_E + p
            hg = _dot(hi, wg_ref[ex])
            hu = _dot(hi, wu_ref[ex])
            cw = combine[:, 8 * p + g:8 * p + g + 1]
            acts.append((hg * _sigmoid(hg) * hu * cw).astype(BF16))
        acc += _dot(jnp.concatenate(acts, axis=1), wd_ref[g])
    o_ref[...] = x + g2_ref[...] * acc


def _moe(x2, sc, sh, g2, gn, wr1, wr2, br, wg, wu, wd, *, tm, rows_per_mod):
    m = x2.shape[0]
    row = lambda w: pl.BlockSpec((tm, w), lambda i: (i, 0))
    ms = _mod_spec(tm, rows_per_mod)
    return pl.pallas_call(
        _moe_kernel,
        out_shape=jax.ShapeDtypeStruct((m, D), F32),
        grid=(m // tm,),
        in_specs=[row(D), ms, ms, ms, _const_spec(gn.shape), _const_spec(wr1.shape), _const_spec(wr2.shape),
                  _const_spec(br.shape), _const_spec(wg.shape), _const_spec(wu.shape), _const_spec(wd.shape)],
        out_specs=row(D),
        compiler_params=_cparams(("arbitrary",)),
        name="moe",
    )(x2, sc, sh, g2, gn, wr1, wr2, br, wg, wu, wd)


def _head_block(nope, r1, r2):
    pad = jnp.zeros(nope.shape[:-1] + (LANE - A_QK,), nope.dtype)
    blk = jnp.concatenate([nope, r1, r2, pad], axis=-1)
    return blk.reshape(blk.shape[:-2] + (AH * LANE,))


def _prep_layer(l, w_in, b_in, w_gate, b_gate, g_norm1, g_norm2, g_sgu, g_mlstm, g_cq, w_q_up, g_ckv, w_uk, w_uv,
                g_qn, g_kn, w_br, w_o, w_exp_gate, w_exp_up, w_exp_down):
    half = A_ROPE // 2
    wi, bi = w_in[l], b_in[l]

    def cat_cols(a, gate):
        z = lambda n: jnp.zeros(a.shape[:-1] + (n,), a.dtype)
        small = jnp.concatenate([z(SM_KR), a[..., 3592:3624], a[..., 3072:3080], z(LANE - SM_F - LH)], axis=-1)
        return jnp.concatenate([a[..., :3072], a[..., 3080:3592], small, gate], axis=-1)

    lw = {}
    lw["wcat"] = cat_cols(wi, w_gate[l]).astype(BF16)
    lw["bcat"] = cat_cols(bi, b_gate[l]).reshape(1, N_CAT)
    lw["g1"] = g_norm1[l].reshape(1, D)
    lw["g2"] = g_norm2[l].reshape(1, D)
    lw["gsgu"] = g_sgu[l].reshape(1, SGU_W)
    lw["gh"] = g_mlstm[l]
    wq = w_q_up[l]
    nope, r1, r2 = wq[..., :A_NOPE], wq[..., A_NOPE:A_NOPE + half], wq[..., A_NOPE + half:]
    lw["wq"] = _head_block(nope, r1, r2).astype(BF16)
    lw["wqr"] = _head_block(jnp.zeros_like(nope), r2, r1).astype(BF16)
    gq = jnp.broadcast_to(g_qn[l], (AH, A_QK))
    gn_, g1_, g2_ = gq[:, :A_NOPE], gq[:, A_NOPE:A_NOPE + half], gq[:, A_NOPE + half:]
    lw["gq"] = _head_block(gn_, g1_, g2_).reshape(1, AH * LANE)
    lw["gqr"] = _head_block(jnp.zeros_like(gn_), g2_, g1_).reshape(1, AH * LANE)
    zr = jnp.zeros((A_KVR, AH, half), F32)
    lw["wuk"] = _head_block(w_uk[l], zr, zr).astype(BF16)
    uv = jnp.concatenate([w_uv[l], jnp.zeros((A_KVR, AH, LANE - A_V), F32)], axis=-1)
    lw["wuv"] = uv.reshape(A_KVR, AH * LANE).astype(BF16)
    lw["wuvp"] = jnp.transpose(uv, (1, 0, 2)).astype(BF16)
    lw["vone"] = jnp.tile((jnp.arange(LANE) == A_V).astype(F32), AH).reshape(1, AH * LANE)
    lw["gcq"] = g_cq[l].reshape(1, A_QR)
    lw["gckv"] = g_ckv[l].reshape(1, A_KVR)
    gk = g_kn[l]
    lw["gkn"] = jnp.concatenate([gk[:A_NOPE], jnp.zeros((LANE - A_NOPE,), F32)]).reshape(1, LANE)
    lw["gkp"] = jnp.concatenate([jnp.zeros((SM_KR,), F32), gk[A_NOPE:], jnp.zeros((LANE - A_QK,), F32)]).reshape(1, LANE)
    ukt = jnp.transpose(w_uk[l], (1, 2, 0))
    lw["wukt"] = jnp.concatenate([ukt, jnp.zeros((AH, LANE - A_NOPE, A_KVR), F32)], axis=1).astype(BF16)
    lw["wa"] = w_br[l, 0].astype(BF16)
    lw["wb"] = w_br[l, 1].astype(BF16)
    wc = w_br[l, 2].reshape(AH, A_V, D)
    lw["wc"] = jnp.concatenate([wc, jnp.zeros((AH, LANE - A_V, D), F32)], axis=1).reshape(AH * LANE, D).astype(BF16)
    lw["wo"] = w_o[l].astype(BF16)
    lw["wg"] = w_exp_gate[l].astype(BF16)
    lw["wu"] = w_exp_up[l].astype(BF16)
    lw["wd"] = w_exp_down[l].reshape(N_GRP, GRP_E * D_EXP, D).astype(BF16)
    return lw


def _prep_router(w_router, b_router):
    wr = w_router.reshape(D, N_GRP, GRP_E).transpose(0, 2, 1)
    wr = jnp.concatenate([wr, jnp.zeros((D, GRP_E, 8 - N_GRP), F32)], axis=-1).reshape(D, 8 * GRP_E)
    hi = wr.astype(BF16)
    lo = (wr - hi.astype(F32)).astype(BF16)
    z = lambda n: jnp.zeros((D, n), BF16)
    wr1 = jnp.concatenate([hi, lo, z(LANE - 64)], axis=1)
    wr2 = jnp.concatenate([hi, z(LANE - 32)], axis=1)
    br = b_router.reshape(N_GRP, GRP_E).T
    br = jnp.concatenate([br, jnp.zeros((GRP_E, 8 - N_GRP), F32)], axis=-1).reshape(8 * GRP_E, 1)
    return wr1, wr2, br


def _rope_tables(pos):
    half = A_ROPE // 2
    freqs = ROPE_BASE ** (-jnp.arange(half, dtype=F32) / half)
    ang = pos[:, None] * freqs[None, :]
    cos, sin = jnp.cos(ang), jnp.sin(ang)
    n = pos.shape[0]
    cos_t = jnp.concatenate([jnp.ones((n, A_NOPE), F32), cos, cos, jnp.zeros((n, LANE - A_QK), F32)], axis=1)
    sin_t = jnp.concatenate([jnp.zeros((n, A_NOPE), F32), -sin, sin, jnp.zeros((n, LANE - A_QK), F32)], axis=1)
    return cos_t, sin_t


def _mix_weights(w_spatial_l, b_spatial_l, chunk, reps):
    w = jnp.where(jnp.tril(jnp.ones((chunk, chunk), dtype=bool))[None], w_spatial_l[:, :chunk, :chunk], 0.0)
    if reps > 1:
        w = jax.vmap(lambda a: jnp.kron(jnp.eye(reps, dtype=F32), a))(w)
    bias = jnp.tile(b_spatial_l[:, :chunk].T, (reps, 1))
    bias = jnp.repeat(bias, SGU_C, axis=1)
    return w.astype(BF16), bias


def _layer_prompt(x2, mod, lw, router, tables, wmix, bmix, b, t):
    m = b * t
    tm = 512
    sh1, sc1, g1, sh2, sc2, g2 = [a.reshape(b, 1, D) for a in jnp.split(mod, 6, axis=-1)]
    oa, q, k, v, og, zc, small, gates = _proj(x2, sc1, sh1, lw["wcat"], lw["bcat"], lw["g1"], lw["gsgu"], wmix, bmix,
                                              tm=tm, rows_per_mod=t, emit_v=False)
    ckv, kpe, ksc, qa, ka, va = _mla_prep(zc, small, tables[0], tables[1], lw, tm=tm, rows_per_pos=t, sample=False)
    n_chunks = t // LCHUNK
    c0 = jnp.zeros((b, LH, LDK, LDV), F32)
    n0 = jnp.zeros((b, LH, LDK), F32)
    m0 = jnp.zeros((b, 1, LANE), F32)
    ob, c_f, n_f, m_f = _mlstm(q, k, v, og, small, lw["gh"], c0, n0, m0, n_chunks=n_chunks)
    oc = _attn_prompt(qa, ka, va, b=b, t=t, tq=512, tk=256)
    x1 = _merge(x2, g1, oa, ob, oc, gates, lw["wa"], lw["wb"], lw["wc"], lw["wo"], tm=tm, rows_per_mod=t)
    x_out = _moe(x1, sc2, sh2, g2, lw["g2"], *router, lw["wg"], lw["wu"], lw["wd"], tm=tm, rows_per_mod=t)
    state = (ckv.reshape(b, t, A_KVR), kpe.reshape(b, t, A_ROPE), ksc.reshape(b, t, AH), c_f, n_f, m_f[:, 0, :LH])
    return x_out, state


def _layer_sample(x2, mod, lw, router, tables, wmix, bmix, b, t, st_c, st_n, st_m, caches, page_table, layer):
    m = b * t
    tm = m
    mods = [jnp.repeat(a, t, axis=0) for a in jnp.split(mod, 6, axis=-1)]
    sh1, sc1, g1, sh2, sc2, g2 = mods
    oa, q, k, v, og, zc, small, gates, v_sgu = _proj(x2, sc1, sh1, lw["wcat"], lw["bcat"], lw["g1"], lw["gsgu"], wmix,
                                                     bmix, tm=tm, rows_per_mod=None, emit_v=True)
    ckv, kpe, ksc, qa, qlat = _mla_prep(zc, small, tables[0], tables[1], lw, tm=tm, rows_per_pos=None, sample=True)

    def pad_rows(a, fill=None):
        a3 = a.reshape(b, t, a.shape[-1])
        if fill is None:
            padv = jnp.zeros((b, LCHUNK - t, a.shape[-1]), a.dtype)
        else:
            padv = jnp.broadcast_to(fill, (b, LCHUNK - t, a.shape[-1])).astype(a.dtype)
        return jnp.concatenate([a3, padv], axis=1).reshape(b * LCHUNK, a.shape[-1])

    lane = jnp.arange(LANE)
    inert = jnp.where((lane >= SM_I) & (lane < SM_I + LH), -1e30, jnp.where((lane >= SM_F) & (lane < SM_F + LH), 1e30, 0.0))
    m0 = jnp.concatenate([st_m, jnp.zeros((b, LANE - LH), F32)], axis=1).reshape(b, 1, LANE)
    ob_p, c_f, n_f, m_f = _mlstm(pad_rows(q), pad_rows(k), pad_rows(v), pad_rows(og), pad_rows(small, inert.astype(F32)),
                                 lw["gh"], st_c, st_n, m0, n_chunks=1)
    ob = ob_p.reshape(b, LCHUNK, 512)[:, :t].reshape(m, 512)

    nr = t * AH
    nk = 8
    ql3 = qlat.reshape(b, nr, A_KVR).astype(BF16)
    qp3 = qa.reshape(b, t, AH, LANE)[..., A_NOPE:A_QK].reshape(b, nr, A_ROPE).astype(BF16)
    padk = lambda a: jnp.concatenate([a.reshape(b, t, -1), jnp.zeros((b, nk - t, a.shape[-1]), F32)], axis=1)
    ks_t = jnp.transpose(padk(ksc), (0, 2, 1))
    ks_t = jnp.tile(ks_t, (1, t, 1))
    kp_t = jnp.transpose(padk(kpe), (0, 2, 1))
    olat = _attn_paged(page_table, ql3, qp3, padk(ckv), kp_t, ks_t, *caches, layer=layer, t_new=t, n_per_step=8)
    olat_h = jnp.transpose(olat.reshape(b, t, AH, A_KVR), (2, 0, 1, 3)).reshape(AH, m, A_KVR)
    oc = _uv_proj(olat_h, lw["wuvp"])

    x1 = _merge(x2, g1, oa, ob, oc, gates, lw["wa"], lw["wb"], lw["wc"], lw["wo"], tm=tm, rows_per_mod=None)
    x_out = _moe(x1, sc2, sh2, g2, lw["g2"], *router, lw["wg"], lw["wu"], lw["wd"], tm=tm, rows_per_mod=None)
    state = (ckv.reshape(b, t, A_KVR), kpe.reshape(b, t, A_ROPE), ksc.reshape(b, t, AH), c_f, n_f, m_f[:, 0, :LH],
             v_sgu.reshape(b, t, SGU_W))
    return x_out, state


def kernel(x_prompt, x_sample, c_prompt, c_sample, cache_ckv, cache_kpe, cache_kscale, page_table, state_C, state_n, state_m, w_ada, b_ada, g_norm1, g_norm2, w_in, b_in, g_sgu, w_spatial, b_spatial, g_mlstm, g_cq, w_q_up, g_ckv, w_uk, w_uv, g_qn, g_kn, w_br, w_gate, b_gate, w_o, w_router, b_router, w_exp_gate, w_exp_up, w_exp_down):
    bp, tp, _ = x_prompt.shape
    bs, ts, _ = x_sample.shape
    depth = w_in.shape[0]
    past_len = page_table.shape[1] * PAGE

    mod_all = _ada(jnp.concatenate([c_prompt, c_sample], axis=0), w_ada, b_ada)
    router = _prep_router(w_router, b_router)
    tab_p = _rope_tables(jnp.arange(tp, dtype=F32))
    pos_s = jnp.tile(jnp.arange(ts, dtype=F32) + past_len, bs)
    tab_s = _rope_tables(pos_s)
    caches = (cache_ckv, jnp.swapaxes(cache_kpe, 2, 3), jnp.swapaxes(cache_kscale, 2, 3))

    xp = x_prompt.reshape(bp * tp, D)
    xs = x_sample.reshape(bs * ts, D)
    st_p, st_s = [], []
    for l in range(depth):
        lw = _prep_layer(l, w_in, b_in, w_gate, b_gate, g_norm1, g_norm2, g_sgu, g_mlstm, g_cq, w_q_up, g_ckv, w_uk,
                         w_uv, g_qn, g_kn, w_br, w_o, w_exp_gate, w_exp_up, w_exp_down)
        chunk_p = min(tp, SGU_C)
        wmix_p, bmix_p = _mix_weights(w_spatial[l], b_spatial[l], chunk_p, 1)
        wmix_s, bmix_s = _mix_weights(w_spatial[l], b_spatial[l], ts, bs)
        xp, sp = _layer_prompt(xp, mod_all[l, :bp], lw, router, tab_p, wmix_p, bmix_p, bp, tp)
        xs, ss = _layer_sample(xs, mod_all[l, bp:], lw, router, tab_s, wmix_s, bmix_s, bs, ts, state_C[l], state_n[l],
                               state_m[l], caches, page_table, l)
        st_p.append(sp)
        st_s.append(ss)
    stack = lambda sts, i: jnp.stack([s[i] for s in sts])
    return (xp.reshape(bp, tp, D), xs.reshape(bs, ts, D),
            stack(st_p, 0), stack(st_p, 1), stack(st_p, 2), stack(st_p, 3), stack(st_p, 4), stack(st_p, 5),
            stack(st_s, 0), stack(st_s, 1), stack(st_s, 2), stack(st_s, 3), stack(st_s, 4), stack(st_s, 5),
            stack(st_s, 6))
```

```python
import functools
import math

import jax
import jax.numpy as jnp
from jax import lax
from jax.experimental import pallas as pl
from jax.experimental.pallas import tpu as pltpu

F32 = jnp.float32
BF16 = jnp.bfloat16
NEG_INF = float("-inf")

EPS = 1e-6
D = 1024
SGU_G = 4
SGU_C = 128
SGU_W = SGU_G * SGU_C
LH = 4
LDK = 128
LDV = 128
LCHUNK = 128
AH = 8
A_QR = 256
A_KVR = 256
A_NOPE = 64
A_ROPE = 32
A_QK = A_NOPE + A_ROPE
A_V = 64
ROPE_BASE = 10000.0
PAGE = 128
N_EXP = 16
N_GRP = 4
GRP_E = 4
D_EXP = 256

LANE = 128
BF16_ROWS = 16
C_ZU, C_ZV, C_ZQ, C_ZK, C_ZVV, C_ZO = 0, 512, 1024, 1536, 2048, 2560
C_ZC = 3072
C_SMALL = 3584
C_GATE = 3712
N_CAT = C_GATE + 3 * D
SM_KR, SM_I, SM_F = 64, 96, 100

VMEM_LIMIT = 56 * 1024 * 1024


def _cparams(sem):
    return pltpu.CompilerParams(dimension_semantics=sem, vmem_limit_bytes=VMEM_LIMIT)


def _const_spec(shape):
    nd = len(shape)
    return pl.BlockSpec(shape, lambda *_: (0,) * nd, pipeline_mode=pl.Buffered(1))


def _sigmoid(x):
    return 0.5 * jnp.tanh(0.5 * x) + 0.5


def _dot(a, b):
    return jnp.dot(a, b, preferred_element_type=F32)


def _dot_nt(a, b):
    return lax.dot_general(a, b, (((1,), (1,)), ((), ())), preferred_element_type=F32)


def _rms(x, g):
    return x * lax.rsqrt(jnp.mean(x * x, axis=-1, keepdims=True) + EPS) * g


def _ada_kernel(c_ref, w_ref, b_ref, o_ref):
    c = c_ref[...]
    s = (c * _sigmoid(c)).astype(BF16)
    o_ref[...] = _dot(s, w_ref[...].astype(BF16)) + b_ref[...]


def _ada(c_all, w_ada, b_ada):
    depth, _, n = w_ada.shape
    bc = c_all.shape[0]
    tn = 1536
    return pl.pallas_call(
        _ada_kernel,
        out_shape=jax.ShapeDtypeStruct((depth, bc, n), F32),
        grid=(depth, n // tn),
        in_specs=[
            pl.BlockSpec((bc, D), lambda l, j: (0, 0)),
            pl.BlockSpec((None, D, tn), lambda l, j: (l, 0, j)),
            pl.BlockSpec((None, 1, tn), lambda l, j: (l, 0, j)),
        ],
        out_specs=pl.BlockSpec((None, bc, tn), lambda l, j: (l, 0, j)),
        compiler_params=_cparams(("arbitrary", "arbitrary")),
        name="ada",
    )(c_all, w_ada, b_ada.reshape(depth, 1, n))


def _proj_kernel(x_ref, sc_ref, sh_ref, w_ref, b_ref, g1_ref, gsgu_ref, wmix_ref, bmix_ref,
                 oa_ref, q_ref, k_ref, v_ref, og_ref, zc_ref, small_ref, gates_ref, *rest, mix_rows, emit_v):
    tm = x_ref.shape[0]
    x = x_ref[...]
    h = _rms(x, g1_ref[...]) * (1.0 + sc_ref[...]) + sh_ref[...]
    hb = h.astype(BF16)

    def sec(a, b):
        return _dot(hb, w_ref[:, a:b]) + b_ref[:, a:b]

    u = jax.nn.gelu(sec(C_ZU, C_ZU + SGU_W))
    vv = _rms(jax.nn.gelu(sec(C_ZV, C_ZV + SGU_W)), gsgu_ref[...])
    if emit_v:
        rest[0][...] = vv
    vb = vv.astype(BF16)
    for g in range(SGU_G):
        cs = slice(g * SGU_C, (g + 1) * SGU_C)
        for c in range(tm // mix_rows):
            rs = slice(c * mix_rows, (c + 1) * mix_rows)
            z = _dot(wmix_ref[g], vb[rs, cs]) + bmix_ref[:, cs]
            oa_ref[rs, cs] = (u[rs, cs] * z).astype(oa_ref.dtype)

    q_ref[...] = sec(C_ZQ, C_ZQ + 512).astype(q_ref.dtype)
    k_ref[...] = (sec(C_ZK, C_ZK + 512) * (LDK ** -0.5)).astype(k_ref.dtype)
    v_ref[...] = sec(C_ZVV, C_ZVV + 512).astype(v_ref.dtype)
    og_ref[...] = _sigmoid(sec(C_ZO, C_ZO + 512)).astype(og_ref.dtype)
    zc_ref[...] = sec(C_ZC, C_ZC + 512)
    small_ref[...] = sec(C_SMALL, C_SMALL + LANE)
    for n in range(3):
        a = C_GATE + n * D
        gates_ref[:, n * D:(n + 1) * D] = _sigmoid(sec(a, a + D)).astype(gates_ref.dtype)


def _proj(x2, sc, sh, wcat, bcat, g1, gsgu, wmix, bmix, *, tm, rows_per_mod, emit_v):
    m = x2.shape[0]
    mix_rows = wmix.shape[-1]
    if rows_per_mod is None:
        mod_spec = pl.BlockSpec((tm, D), lambda i: (i, 0))
    else:
        tpb = rows_per_mod // tm
        mod_spec = pl.BlockSpec((None, 1, D), lambda i: (i // tpb, 0, 0))
    row = lambda w: pl.BlockSpec((tm, w), lambda i: (i, 0))
    out_shape = [
        jax.ShapeDtypeStruct((m, 512), BF16),
        jax.ShapeDtypeStruct((m, 512), BF16),
        jax.ShapeDtypeStruct((m, 512), BF16),
        jax.ShapeDtypeStruct((m, 512), BF16),
        jax.ShapeDtypeStruct((m, 512), BF16),
        jax.ShapeDtypeStruct((m, 512), F32),
        jax.ShapeDtypeStruct((m, LANE), F32),
        jax.ShapeDtypeStruct((m, 3 * D), BF16),
    ]
    out_specs = [row(512)] * 5 + [row(512), row(LANE), row(3 * D)]
    if emit_v:
        out_shape.append(jax.ShapeDtypeStruct((m, SGU_W), F32))
        out_specs.append(row(SGU_W))
    return pl.pallas_call(
        functools.partial(_proj_kernel, mix_rows=mix_rows, emit_v=emit_v),
        out_shape=out_shape,
        grid=(m // tm,),
        in_specs=[
            row(D), mod_spec, mod_spec,
            _const_spec(wcat.shape), _const_spec(bcat.shape), _const_spec(g1.shape), _const_spec(gsgu.shape),
            _const_spec(wmix.shape), _const_spec(bmix.shape),
        ],
        out_specs=out_specs,
        compiler_params=_cparams(("arbitrary",)),
        name="proj",
    )(x2, sc, sh, wcat, bcat, g1, gsgu, wmix, bmix)


def _mla_kernel(zc_ref, small_ref, cos_ref, sin_ref, wq_ref, wqr_ref, gq_ref, gqr_ref, wuk_ref, wuv_ref, vone_ref,
                gcq_ref, gckv_ref, gkn_ref, gkp_ref, wukt_ref,
                ckv_ref, kpe_ref, ksc_ref, *outs, sample):
    tm = zc_ref.shape[0]
    cosv = cos_ref[...]
    sinv = sin_ref[...]
    lane = lax.broadcasted_iota(jnp.int32, (tm, LANE), 1)
    scale = A_QK ** -0.5

    cq = _rms(zc_ref[:, 0:A_QR], gcq_ref[...]).astype(BF16)
    q = _dot(cq, wq_ref[...])
    qr = _dot(cq, wqr_ref[...])

    ckv = _rms(zc_ref[:, A_QR:A_QR + A_KVR], gckv_ref[...])
    ckv_ref[...] = ckv
    ckvb = ckv.astype(BF16)
    kn = _dot(ckvb, wuk_ref[...])

    small = small_ref[...]
    zkr = jnp.where((lane >= SM_KR) & (lane < SM_KR + A_ROPE), small, 0.0)
    t = zkr * gkp_ref[...]
    half = A_ROPE // 2
    rot = jnp.where(lane < SM_KR + half, pltpu.roll(t, LANE - half, 1), pltpu.roll(t, half, 1))
    kpe = t * cosv + rot * sinv
    kpe_ref[...] = kpe[:, SM_KR:SM_KR + A_ROPE]

    ksc = jnp.zeros((tm, LANE), F32)
    if sample:
        qa_ref, ql_ref = outs
    else:
        qa_ref, ka_ref, vt_ref = outs
        vt_ref[...] = (_dot_nt(wuv_ref[...], ckvb) + vone_ref[...]).astype(vt_ref.dtype)
    for h in range(AH):
        hs = slice(h * LANE, (h + 1) * LANE)
        qb = q[:, hs]
        rinv = lax.rsqrt(jnp.sum(qb * qb, axis=-1, keepdims=True) / A_QK + EPS)
        qh = (qb * rinv * gq_ref[:, hs] * cosv + qr[:, hs] * rinv * gqr_ref[:, hs] * sinv)
        kraw = kn[:, hs] + zkr
        ks = lax.rsqrt(jnp.sum(kraw * kraw, axis=-1, keepdims=True) / A_QK + EPS)
        ksc = jnp.where(lane == h, ks, ksc)
        if sample:
            qa_ref[:, hs] = qh * scale
            qn = (qh * gkn_ref[...]).astype(BF16)
            ql_ref[:, h * A_KVR:(h + 1) * A_KVR] = _dot(qn, wukt_ref[h]) * scale
        else:
            qa_ref[:, hs] = (qh * scale).astype(qa_ref.dtype)
            ka_ref[:, hs] = ((kn[:, hs] * gkn_ref[...] + kpe) * ks).astype(ka_ref.dtype)
    ksc_ref[...] = ksc[:, 0:AH]


def _mla_prep(zc, small, cos_t, sin_t, lw, *, tm, rows_per_pos, sample):
    m = zc.shape[0]
    if rows_per_pos is None:
        pos_spec = pl.BlockSpec((tm, LANE), lambda i: (i, 0))
    else:
        tpb = rows_per_pos // tm
        pos_spec = pl.BlockSpec((tm, LANE), lambda i: (i % tpb, 0))
    row = lambda w: pl.BlockSpec((tm, w), lambda i: (i, 0))
    out_shape = [
        jax.ShapeDtypeStruct((m, A_KVR), F32),
        jax.ShapeDtypeStruct((m, A_ROPE), F32),
        jax.ShapeDtypeStruct((m, AH), F32),
    ]
    out_specs = [row(A_KVR), row(A_ROPE), row(AH)]
    if sample:
        out_shape += [jax.ShapeDtypeStruct((m, AH * LANE), F32), jax.ShapeDtypeStruct((m, AH * A_KVR), F32)]
        out_specs += [row(AH * LANE), row(AH * A_KVR)]
    else:
        n_seq = m // rows_per_pos
        out_shape += [jax.ShapeDtypeStruct((m, AH * LANE), BF16)] * 2
        out_shape += [jax.ShapeDtypeStruct((n_seq * AH * LANE, rows_per_pos), BF16)]
        out_specs += [row(AH * LANE)] * 2
        out_specs += [pl.BlockSpec((AH * LANE, tm), lambda i: (i // tpb, i % tpb))]
    names =("wq", "wqr", "gq", "gqr", "wuk", "wuv", "vone", "gcq", "gckv", "gkn", "gkp", "wukt")
    ws = [lw[n] for n in names]
    return pl.pallas_call(
        functools.partial(_mla_kernel, sample=sample),
        out_shape=out_shape,
        grid=(m // tm,),
        in_specs=[row(512), row(LANE), pos_spec, pos_spec] + [_const_spec(w.shape) for w in ws],
        out_specs=out_specs,
        compiler_params=_cparams(("arbitrary",)),
        name="mla_prep",
    )(zc, small, cos_t, sin_t, *ws)


def _mlstm_kernel(q_ref, k_ref, v_ref, og_ref, small_ref, gh_ref, c0_ref, n0_ref, m0_ref,
                  h_ref, c_ref, n_ref, m_ref):
    L = q_ref.shape[0]

    @pl.when(pl.program_id(1) == 0)
    def _():
        c_ref[...] = c0_ref[...]
        n_ref[...] = n0_ref[...]
        m_ref[...] = m0_ref[...]

    small = small_ref[...]
    row = lax.broadcasted_iota(jnp.int32, (L, L), 0)
    col = lax.broadcasted_iota(jnp.int32, (L, L), 1)
    causal = col <= row
    eye = col == row
    lane = lax.broadcasted_iota(jnp.int32, (1, LANE), 1)
    m_all = m_ref[...]
    m_out = m_all
    for h in range(LH):
        hs = slice(h * LDK, (h + 1) * LDK)
        i_col = small[:, SM_I + h:SM_I + h + 1]
        zf = small[:, SM_F + h:SM_F + h + 1]
        lf_col = jnp.minimum(zf, 0.0) - jnp.log1p(jnp.exp(-jnp.abs(zf)))
        b_row = jnp.sum(jnp.where(row <= col, lf_col, 0.0), axis=0, keepdims=True)
        b_col = jnp.sum(jnp.where(eye, b_row, 0.0), axis=1, keepdims=True)
        i_row = jnp.sum(jnp.where(eye, i_col, 0.0), axis=0, keepdims=True)
        logw = jnp.where(causal, b_col - b_row + i_row, NEG_INF)
        m_prev = m_all[:, h:h + 1]
        m_inter = m_prev + b_col
        m_t = jnp.maximum(m_inter, jnp.max(logw, axis=1, keepdims=True))
        qh = q_ref[:, hs]
        kh = k_ref[:, hs]
        vh = v_ref[:, hs]
        smat = _dot_nt(qh, kh) * jnp.exp(logw - m_t)
        a_inter = jnp.exp(m_inter - m_t)
        c_h = c_ref[h]
        n_h = n_ref[h:h + 1, :]
        num = _dot(smat.astype(BF16), vh) + a_inter * _dot(qh, c_h.astype(BF16))
        den = jnp.sum(smat, axis=1, keepdims=True) + a_inter * jnp.sum(qh.astype(F32) * n_h, axis=1, keepdims=True)
        hv = num / jnp.maximum(jnp.abs(den), jnp.exp(-m_t))
        hn = _rms(hv, gh_ref[h:h + 1, :])
        h_ref[:, hs] = (og_ref[:, hs].astype(F32) * hn).astype(h_ref.dtype)
        m_new = m_t[L - 1:L, :]
        b_last = b_col[L - 1:L, :]
        decay = jnp.exp(m_prev + b_last - m_new)
        w_col = jnp.exp(b_last - b_col + i_col - m_new)
        wk = kh.astype(F32) * w_col
        c_ref[h] = decay * c_h + lax.dot_general(wk.astype(BF16), vh, (((0,), (0,)), ((), ())),
                                                 preferred_element_type=F32)
        n_ref[h:h + 1, :] = decay * n_h + jnp.sum(wk, axis=0, keepdims=True)
        m_out = jnp.where(lane == h, m_new, m_out)
    m_ref[...] = m_out


def _mlstm(q, k, v, og, small, gh, c0, n0, m0, *, n_chunks):
    m = q.shape[0]
    b = c0.shape[0]
    L = m // (b * n_chunks)
    row = lambda w: pl.BlockSpec((L, w), lambda i, c: (i * n_chunks + c, 0))
    st_c = pl.BlockSpec((None, LH, LDK, LDV), lambda i, c: (i, 0, 0, 0))
    st_n = pl.BlockSpec((None, LH, LDK), lambda i, c: (i, 0, 0))
    st_m = pl.BlockSpec((None, 1, LANE), lambda i, c: (i, 0, 0))
    return pl.pallas_call(
        _mlstm_kernel,
        out_shape=[
            jax.ShapeDtypeStruct((m, 512), BF16),
            jax.ShapeDtypeStruct((b, LH, LDK, LDV), F32),
            jax.ShapeDtypeStruct((b, LH, LDK), F32),
            jax.ShapeDtypeStruct((b, 1, LANE), F32),
        ],
        grid=(b, n_chunks),
        in_specs=[row(512), row(512), row(512), row(512), row(LANE), _const_spec(gh.shape), st_c, st_n, st_m],
        out_specs=[row(512), st_c, st_n, st_m],
        compiler_params=_cparams(("arbitrary", "arbitrary")),
        name="mlstm",
    )(q, k, v, og, small, gh, c0, n0, m0)


def _attn_kernel(q_ref, k_ref, vt_ref, o_ref, p_sc, *, tq):
    t = q_ref.shape[0]
    key = lax.broadcasted_iota(jnp.int32, (tq, tq), 0)
    qry = lax.broadcasted_iota(jnp.int32, (tq, tq), 1)
    for i in range(t // tq):
        r0, r1 = i * tq, (i + 1) * tq
        st = _dot_nt(k_ref[0:r1, :], q_ref[r0:r1, :])
        st_d = jnp.where(key <= qry, st[r0:], NEG_INF)
        m = jnp.max(st_d, axis=0, keepdims=True)
        if i > 0:
            m = jnp.maximum(m, jnp.max(st[:r0], axis=0, keepdims=True))
            p_sc[0:r0] = jnp.exp(st[:r0] - m).astype(BF16)
        p_sc[r0:r1] = jnp.exp(st_d - m).astype(BF16)
        ot = _dot(vt_ref[:, 0:r1], p_sc[0:r1])
        o_ref[r0:r1, :] = (ot / ot[A_V:A_V + 1, :]).T.astype(o_ref.dtype)


def _attn_prompt(qa, ka, vt, *, b, t, tq):
    blk = pl.BlockSpec((t, LANE), lambda i, h: (i, h))
    return pl.pallas_call(
        functools.partial(_attn_kernel, tq=tq),
        out_shape=jax.ShapeDtypeStruct(qa.shape, BF16),
        grid=(b, AH),
        in_specs=[blk, blk, pl.BlockSpec((LANE, t), lambda i, h: (i * AH + h, 0))],
        out_specs=blk,
        scratch_shapes=[pltpu.VMEM((t, tq), BF16)],
        compiler_params=_cparams(("arbitrary", "arbitrary")),
        name="attn_prompt",
    )(qa, ka, vt)


def _paged_kernel(pt_ref, ql_ref, qp_ref, cn_ref, kpn_ref, ksn_ref, ckv_hbm, kpe_hbm, ks_hbm,
                  o_ref, cbuf, kpbuf, ksbuf, sems, m_sc, l_sc, acc_sc, *, layer, n_per_step, n_chains, t_new):
    j = pl.program_id(1)
    steps = pl.num_programs(1)
    step = pl.program_id(0) * steps + j
    last = pl.num_programs(0) * steps - 1
    slot = step % 2
    ql = ql_ref[...]
    qp = qp_ref[...]
    nr = ql.shape[0]

    def page_copies(step_, slot_, n):
        pg = pt_ref[step_ * n_per_step + n]
        return (pltpu.make_async_copy(ckv_hbm.at[layer, pg], cbuf.at[slot_, n], sems.at[slot_, 0]),
                pltpu.make_async_copy(kpe_hbm.at[layer, pg], kpbuf.at[slot_, n], sems.at[slot_, 1]),
                pltpu.make_async_copy(ks_hbm.at[layer, pg], ksbuf.at[slot_, n], sems.at[slot_, 2]))

    @pl.when(step == 0)
    def _():
        for n in range(n_per_step):
            for cp in page_copies(0, 0, n):
                cp.start()

    for n in range(n_per_step):
        for cp in page_copies(step, slot, n):
            cp.wait()

    @pl.when(j == 0)
    def _():
        cn = cn_ref[...].astype(BF16)
        s = (_dot_nt(ql, cn) + _dot(qp, kpn_ref[...].astype(BF16))) * ksn_ref[...]
        nk = cn.shape[0]
        r_t = lax.broadcasted_iota(jnp.int32, (nr, nk), 0) // AH
        key = lax.broadcasted_iota(jnp.int32, (nr, nk), 1)
        s = jnp.where((key <= r_t) & (key < t_new), s, NEG_INF)
        m0 = jnp.max(s, axis=-1, keepdims=True)
        p = jnp.exp(s - m0)
        m_sc[0] = m0
        l_sc[0] = jnp.sum(p, axis=-1, keepdims=True)
        acc_sc[0] = _dot(p.astype(BF16), cn)
        for c in range(1, n_chains):
            m_sc[c] = m0
            l_sc[c] = jnp.zeros_like(m0)
            acc_sc[c] = jnp.zeros((nr, A_KVR), F32)

    nxt = jnp.minimum(step + 1, last)
    per_chain = n_per_step // n_chains
    cbs, ss = [], []
    for n in range(n_per_step):
        for cp in page_copies(nxt, 1 - slot, n):
            cp.start()
        cb = cbuf[slot, n].astype(BF16)
        kpt = kpbuf[slot, n].astype(BF16)
        ks = ksbuf[slot, n]
        kst = jnp.concatenate([ks] * t_new, axis=0)
        cbs.append(cb)
        ss.append((_dot_nt(ql, cb) + _dot(qp, kpt)) * kst)
    finals = []
    for c in range(n_chains):
        sc = ss[c * per_chain:(c + 1) * per_chain]
        cc = cbs[c * per_chain:(c + 1) * per_chain]
        m_i = m_sc[c]
        m_new = jnp.maximum(m_i, jnp.max(functools.reduce(jnp.maximum, sc), axis=-1, keepdims=True))
        alpha = jnp.exp(m_i - m_new)
        ps = [jnp.exp(s - m_new) for s in sc]
        pv = functools.reduce(jnp.add, [_dot(p.astype(BF16), cb) for p, cb in zip(ps, cc)])
        l_i = alpha * l_sc[c] + jnp.sum(functools.reduce(jnp.add, ps), axis=-1, keepdims=True)
        acc = alpha * acc_sc[c] + pv
        m_sc[c] = m_new
        l_sc[c] = l_i
        acc_sc[c] = acc
        finals.append((m_new, l_i, acc))

    @pl.when(j == steps - 1)
    def _():
        m_f = functools.reduce(jnp.maximum, [f[0] for f in finals])
        ws = [jnp.exp(f[0] - m_f) for f in finals]
        l_f = functools.reduce(jnp.add, [w * f[1] for w, f in zip(ws, finals)])
        acc_f = functools.reduce(jnp.add, [w * f[2] for w, f in zip(ws, finals)])
        o_ref[...] = acc_f / l_f

    @pl.when(step == last)
    def _():
        for n in range(n_per_step):
            for cp in page_copies(last, 1 - slot, n):
                cp.wait()


def _attn_paged(page_table, qlat, qp, c_new, kp_new_t, ks_new_t, cache_ckv, cache_kpe_t, cache_ks_t, *, layer, t_new,
                n_per_step, n_chains):
    bs, nr, _ = qlat.shape
    n_pages = page_table.shape[1]
    nk = c_new.shape[1]
    steps = n_pages // n_per_step
    pt_flat = page_table.reshape(-1)

    per_b = lambda s1, s2: pl.BlockSpec((None, s1, s2), lambda b, j, pt: (b, 0, 0))
    hbm = pl.BlockSpec(memory_space=pl.ANY)
    grid_spec = pltpu.PrefetchScalarGridSpec(
        num_scalar_prefetch=1,
        grid=(bs, steps),
        in_specs=[per_b(nr, A_KVR), per_b(nr, A_ROPE), per_b(nk, A_KVR), per_b(A_ROPE, nk), per_b(nr, nk),
                  hbm, hbm, hbm],
        out_specs=per_b(nr, A_KVR),
        scratch_shapes=[
            pltpu.VMEM((2, n_per_step, PAGE, A_KVR), F32),
            pltpu.VMEM((2, n_per_step, A_ROPE, PAGE), F32),
            pltpu.VMEM((2, n_per_step, AH, PAGE), F32),
            pltpu.SemaphoreType.DMA((2, 3)),
            pltpu.VMEM((n_chains, nr, 1), F32), pltpu.VMEM((n_chains, nr, 1), F32),
            pltpu.VMEM((n_chains, nr, A_KVR), F32),
        ],
    )
    return pl.pallas_call(
        functools.partial(_paged_kernel, layer=layer, n_per_step=n_per_step, n_chains=n_chains, t_new=t_new),
        out_shape=jax.ShapeDtypeStruct((bs, nr, A_KVR), F32),
        grid_spec=grid_spec,
        compiler_params=_cparams(("arbitrary", "arbitrary")),
        name="attn_paged",
    )(pt_flat, qlat, qp, c_new, kp_new_t, ks_new_t, cache_ckv, cache_kpe_t, cache_ks_t)


def _uv_kernel(x_ref, w_ref, o_ref):
    o_ref[...] = _dot(x_ref[...].astype(BF16), w_ref[...]).astype(o_ref.dtype)


def _uv_proj(olat_h, wuvp):
    _, m, _ = olat_h.shape
    return pl.pallas_call(
        _uv_kernel,
        out_shape=jax.ShapeDtypeStruct((m, AH * LANE), BF16),
        grid=(AH,),
        in_specs=[pl.BlockSpec((None, m, A_KVR), lambda h: (h, 0, 0)),
                  pl.BlockSpec((None, A_KVR, LANE), lambda h: (h, 0, 0))],
        out_specs=pl.BlockSpec((m, LANE), lambda h: (0, h)),
        compiler_params=_cparams(("arbitrary",)),
        name="uv_proj",
    )(olat_h, wuvp)


def _merge_kernel(x_ref, g1_ref, oa_ref, ob_ref, oc_ref, gates_ref, wa_ref, wb_ref, wc_ref, wo_ref, o_ref):
    mix = gates_ref[:, 0:D].astype(F32) * _dot(oa_ref[...], wa_ref[...])
    mix += gates_ref[:, D:2 * D].astype(F32) * _dot(ob_ref[...], wb_ref[...])
    mix += gates_ref[:, 2 * D:3 * D].astype(F32) * _dot(oc_ref[...], wc_ref[...])
    o_ref[...] = x_ref[...] + g1_ref[...] * _dot(mix.astype(BF16), wo_ref[...])


def _mod_spec(tm, rows_per_mod):
    if rows_per_mod is None:
        return pl.BlockSpec((tm, D), lambda i: (i, 0))
    tpb = rows_per_mod // tm
    return pl.BlockSpec((None, 1, D), lambda i: (i // tpb, 0, 0))


def _merge(x2, g1, oa, ob, oc, gates, wa, wb, wc, wo, *, tm, rows_per_mod):
    m = x2.shape[0]
    row = lambda w: pl.BlockSpec((tm, w), lambda i: (i, 0))
    return pl.pallas_call(
        _merge_kernel,
        out_shape=jax.ShapeDtypeStruct((m, D), F32),
        grid=(m // tm,),
        in_specs=[row(D), _mod_spec(tm, rows_per_mod), row(512), row(512), row(AH * LANE), row(3 * D),
                  _const_spec(wa.shape), _const_spec(wb.shape), _const_spec(wc.shape), _const_spec(wo.shape)],
        out_specs=row(D),
        compiler_params=_cparams(("arbitrary",)),
        name="merge",
    )(x2, g1, oa, ob, oc, gates, wa, wb, wc, wo)


def _moe_kernel(x_ref, sc_ref, sh_ref, g2_ref, gn_ref, wr1_ref, wr2_ref, br_ref, wg_ref, wu_ref, wd_ref, o_ref):
    tm = x_ref.shape[0]
    x = x_ref[...]
    h2 = _rms(x, gn_ref[...]) * (1.0 + sc_ref[...]) + sh_ref[...]
    hi = h2.astype(BF16)
    lo = (h2 - hi.astype(F32)).astype(BF16)
    r = _dot(hi, wr1_ref[...]) + _dot(lo, wr2_ref[...])
    rt = r.T
    logits = rt[0:32] + rt[32:64]
    scores = _sigmoid(logits)
    sel = scores + br_ref[...]
    e = [sel[8 * p:8 * p + 8] for p in range(GRP_E)]
    sc = [scores[8 * p:8 * p + 8] for p in range(GRP_E)]
    a, b = jnp.maximum(e[0], e[1]), jnp.minimum(e[0], e[1])
    c, d = jnp.maximum(e[2], e[3]), jnp.minimum(e[2], e[3])
    top1 = jnp.maximum(a, c)
    top2 = jnp.maximum(jnp.minimum(a, c), jnp.maximum(b, d))
    rowi = lax.broadcasted_iota(jnp.int32, (8, tm), 0).astype(F32)
    gs = jnp.where(rowi < N_GRP, top1 + top2, NEG_INF)
    gmax = jnp.max(gs, axis=0, keepdims=True)
    gidx = jnp.min(jnp.where(gs == gmax, rowi, 8.0), axis=0, keepdims=True)
    oh = rowi == gidx
    s = [jnp.sum(jnp.where(oh, e[p], 0.0), axis=0, keepdims=True) for p in range(GRP_E)]
    w = [jnp.sum(jnp.where(oh, sc[p], 0.0), axis=0, keepdims=True) for p in range(GRP_E)]

    def first_argmax(vals):
        best = jnp.maximum(jnp.maximum(vals[0], vals[1]), jnp.maximum(vals[2], vals[3]))
        return jnp.where(vals[0] == best, 0, jnp.where(vals[1] == best, 1, jnp.where(vals[2] == best, 2, 3)))

    i1 = first_argmax(s)
    i2 = first_argmax([jnp.where(i1 == p, NEG_INF, s[p]) for p in range(GRP_E)])
    w1 = sum(jnp.where(i1 == p, w[p], 0.0) for p in range(GRP_E))
    w2 = sum(jnp.where(i2 == p, w[p], 0.0) for p in range(GRP_E))
    den = w1 + w2
    w1 = w1 / den
    w2 = w2 / den
    comb = [jnp.where(oh, jnp.where(i1 == p, w1, 0.0) + jnp.where(i2 == p, w2, 0.0), 0.0) for p in range(GRP_E)]
    comb_t = jnp.concatenate(comb + [jnp.zeros((LANE - 8 * GRP_E, tm), F32)], axis=0)
    combine = comb_t.T

    acc = jnp.zeros((tm, D), F32)
    for g in range(N_GRP):
        acts = []
        for p in range(GRP_E):
            ex = g * GRP_E + p
            hg = _dot(hi, wg_ref[ex])
            hu = _dot(hi, wu_ref[ex])
            cw = combine[:, 8 * p + g:8 * p + g + 1]
            acts.append((hg * _sigmoid(hg) * hu * cw).astype(BF16))
        acc += _dot(jnp.concatenate(acts, axis=1), wd_ref[g])
    o_ref[...] = x + g2_ref[...] * acc


def _moe(x2, sc, sh, g2, gn, wr1, wr2, br, wg, wu, wd, *, tm, rows_per_mod):
    m = x2.shape[0]
    row = lambda w: pl.BlockSpec((tm, w), lambda i: (i, 0))
    ms = _mod_spec(tm, rows_per_mod)
    return pl.pallas_call(
        _moe_kernel,
        out_shape=jax.ShapeDtypeStruct((m, D), F32),
        grid=(m // tm,),
        in_specs=[row(D), ms, ms, ms, _const_spec(gn.shape), _const_spec(wr1.shape), _const_spec(wr2.shape),
                  _const_spec(br.shape), _const_spec(wg.shape), _const_spec(wu.shape), _const_spec(wd.shape)],
        out_specs=row(D),
        compiler_params=_cparams(("arbitrary",)),
        name="moe",
    )(x2, sc, sh, g2, gn, wr1, wr2, br, wg, wu, wd)


def _head_block(nope, r1, r2):
    pad = jnp.zeros(nope.shape[:-1] + (LANE - A_QK,), nope.dtype)
    blk = jnp.concatenate([nope, r1, r2, pad], axis=-1)
    return blk.reshape(blk.shape[:-2] + (AH * LANE,))


def _prep_layer(l, w_in, b_in, w_gate, b_gate, g_norm1, g_norm2, g_sgu, g_mlstm, g_cq, w_q_up, g_ckv, w_uk, w_uv,
                g_qn, g_kn, w_br, w_o, w_exp_gate, w_exp_up, w_exp_down):
    half = A_ROPE // 2
    wi, bi = w_in[l], b_in[l]

    def cat_cols(a, gate):
        z = lambda n: jnp.zeros(a.shape[:-1] + (n,), a.dtype)
        small = jnp.concatenate([z(SM_KR), a[..., 3592:3624], a[..., 3072:3080], z(LANE - SM_F - LH)], axis=-1)
        return jnp.concatenate([a[..., :3072], a[..., 3080:3592], small, gate], axis=-1)

    lw = {}
    lw["wcat"] = cat_cols(wi, w_gate[l]).astype(BF16)
    lw["bcat"] = cat_cols(bi, b_gate[l]).reshape(1, N_CAT)
    lw["g1"] = g_norm1[l].reshape(1, D)
    lw["g2"] = g_norm2[l].reshape(1, D)
    lw["gsgu"] = g_sgu[l].reshape(1, SGU_W)
    lw["gh"] = g_mlstm[l]
    wq = w_q_up[l]
    nope, r1, r2 = wq[..., :A_NOPE], wq[..., A_NOPE:A_NOPE + half], wq[..., A_NOPE + half:]
    lw["wq"] = _head_block(nope, r1, r2).astype(BF16)
    lw["wqr"] = _head_block(jnp.zeros_like(nope), r2, r1).astype(BF16)
    gq = jnp.broadcast_to(g_qn[l], (AH, A_QK))
    gn_, g1_, g2_ = gq[:, :A_NOPE], gq[:, A_NOPE:A_NOPE + half], gq[:, A_NOPE + half:]
    lw["gq"] = _head_block(gn_, g1_, g2_).reshape(1, AH * LANE)
    lw["gqr"] = _head_block(jnp.zeros_like(gn_), g2_, g1_).reshape(1, AH * LANE)
    zr = jnp.zeros((A_KVR, AH, half), F32)
    lw["wuk"] = _head_block(w_uk[l], zr, zr).astype(BF16)
    uv = jnp.concatenate([w_uv[l], jnp.zeros((A_KVR, AH, LANE - A_V), F32)], axis=-1)
    lw["wuv"] = uv.reshape(A_KVR, AH * LANE).T.astype(BF16)
    lw["wuvp"] = jnp.transpose(uv, (1, 0, 2)).astype(BF16)
    lw["vone"] = jnp.tile((jnp.arange(LANE) == A_V).astype(F32), AH).reshape(AH * LANE, 1)
    lw["gcq"] = g_cq[l].reshape(1, A_QR)
    lw["gckv"] = g_ckv[l].reshape(1, A_KVR)
    gk = g_kn[l]
    lw["gkn"] = jnp.concatenate([gk[:A_NOPE], jnp.zeros((LANE - A_NOPE,), F32)]).reshape(1, LANE)
    lw["gkp"] = jnp.concatenate([jnp.zeros((SM_KR,), F32), gk[A_NOPE:], jnp.zeros((LANE - A_QK,), F32)]).reshape(1, LANE)
    ukt = jnp.transpose(w_uk[l], (1, 2, 0))
    lw["wukt"] = jnp.concatenate([ukt, jnp.zeros((AH, LANE - A_NOPE, A_KVR), F32)], axis=1).astype(BF16)
    lw["wa"] = w_br[l, 0].astype(BF16)
    lw["wb"] = w_br[l, 1].astype(BF16)
    wc = w_br[l, 2].reshape(AH, A_V, D)
    lw["wc"] = jnp.concatenate([wc, jnp.zeros((AH, LANE - A_V, D), F32)], axis=1).reshape(AH * LANE, D).astype(BF16)
    lw["wo"] = w_o[l].astype(BF16)
    lw["wg"] = w_exp_gate[l].astype(BF16)
    lw["wu"] = w_exp_up[l].astype(BF16)
    lw["wd"] = w_exp_down[l].reshape(N_GRP, GRP_E * D_EXP, D).astype(BF16)
    return lw


def _prep_router(w_router, b_router):
    wr = w_router.reshape(D, N_GRP, GRP_E).transpose(0, 2, 1)
    wr = jnp.concatenate([wr, jnp.zeros((D, GRP_E, 8 - N_GRP), F32)], axis=-1).reshape(D, 8 * GRP_E)
    hi = wr.astype(BF16)
    lo = (wr - hi.astype(F32)).astype(BF16)
    z = lambda n: jnp.zeros((D, n), BF16)
    wr1 = jnp.concatenate([hi, lo, z(LANE - 64)], axis=1)
    wr2 = jnp.concatenate([hi, z(LANE - 32)], axis=1)
    br = b_router.reshape(N_GRP, GRP_E).T
    br = jnp.concatenate([br, jnp.zeros((GRP_E, 8 - N_GRP), F32)], axis=-1).reshape(8 * GRP_E, 1)
    return wr1, wr2, br


def _rope_tables(pos):
    half = A_ROPE // 2
    freqs = ROPE_BASE ** (-jnp.arange(half, dtype=F32) / half)
    ang = pos[:, None] * freqs[None, :]
    cos, sin = jnp.cos(ang), jnp.sin(ang)
    n = pos.shape[0]
    cos_t = jnp.concatenate([jnp.ones((n, A_NOPE), F32), cos, cos, jnp.zeros((n, LANE - A_QK), F32)], axis=1)
    sin_t = jnp.concatenate([jnp.zeros((n, A_NOPE), F32), -sin, sin, jnp.zeros((n, LANE - A_QK), F32)], axis=1)
    return cos_t, sin_t


def _mix_weights(w_spatial_l, b_spatial_l, chunk, reps):
    w = jnp.where(jnp.tril(jnp.ones((chunk, chunk), dtype=bool))[None], w_spatial_l[:, :chunk, :chunk], 0.0)
    if reps > 1:
        w = jax.vmap(lambda a: jnp.kron(jnp.eye(reps, dtype=F32), a))(w)
    bias = jnp.tile(b_spatial_l[:, :chunk].T, (reps, 1))
    bias = jnp.repeat(bias, SGU_C, axis=1)
    return w.astype(BF16), bias


def _layer_prompt(x2, mod, lw, router, tables, wmix, bmix, b, t):
    m = b * t
    tm = 512
    sh1, sc1, g1, sh2, sc2, g2 = [a.reshape(b, 1, D) for a in jnp.split(mod, 6, axis=-1)]
    oa, q, k, v, og, zc, small, gates = _proj(x2, sc1, sh1, lw["wcat"], lw["bcat"], lw["g1"], lw["gsgu"], wmix, bmix,
                                              tm=tm, rows_per_mod=t, emit_v=False)
    ckv, kpe, ksc, qa, ka, va = _mla_prep(zc, small, tables[0], tables[1], lw, tm=tm, rows_per_pos=t, sample=False)
    n_chunks = t // LCHUNK
    c0 = jnp.zeros((b, LH, LDK, LDV), F32)
    n0 = jnp.zeros((b, LH, LDK), F32)
    m0 = jnp.zeros((b, 1, LANE), F32)
    ob, c_f, n_f, m_f = _mlstm(q, k, v, og, small, lw["gh"], c0, n0, m0, n_chunks=n_chunks)
    oc = _attn_prompt(qa, ka, va, b=b, t=t, tq=min(t, 256))
    x1 = _merge(x2, g1, oa, ob, oc, gates, lw["wa"], lw["wb"], lw["wc"], lw["wo"], tm=tm, rows_per_mod=t)
    x_out = _moe(x1, sc2, sh2, g2, lw["g2"], *router, lw["wg"], lw["wu"], lw["wd"], tm=tm, rows_per_mod=t)
    state = (ckv.reshape(b, t, A_KVR), kpe.reshape(b, t, A_ROPE), ksc.reshape(b, t, AH), c_f, n_f, m_f[:, 0, :LH])
    return x_out, state


def _layer_sample(x2, mod, lw, router, tables, wmix, bmix, b, t, st_c, st_n, st_m, caches, page_table, layer):
    m = b * t
    tm = m
    mods = [jnp.repeat(a, t, axis=0) for a in jnp.split(mod, 6, axis=-1)]
    sh1, sc1, g1, sh2, sc2, g2 = mods
    oa, q, k, v, og, zc, small, gates, v_sgu = _proj(x2, sc1, sh1, lw["wcat"], lw["bcat"], lw["g1"], lw["gsgu"], wmix,
                                                     bmix, tm=tm, rows_per_mod=None, emit_v=True)
    ckv, kpe, ksc, qa, qlat = _mla_prep(zc, small, tables[0], tables[1], lw, tm=tm, rows_per_pos=None, sample=True)

    lpad = -(-t // BF16_ROWS) * BF16_ROWS

    def pad_rows(a, fill=None):
        a3 = a.reshape(b, t, a.shape[-1])
        if fill is None:
            padv = jnp.zeros((b, lpad - t, a.shape[-1]), a.dtype)
        else:
            padv = jnp.broadcast_to(fill, (b, lpad - t, a.shape[-1])).astype(a.dtype)
        return jnp.concatenate([a3, padv], axis=1).reshape(b * lpad, a.shape[-1])

    lane = jnp.arange(LANE)
    inert = jnp.where((lane >= SM_I) & (lane < SM_I + LH), -1e30, jnp.where((lane >= SM_F) & (lane < SM_F + LH), 1e30, 0.0))
    m0 = jnp.concatenate([st_m, jnp.zeros((b, LANE - LH), F32)], axis=1).reshape(b, 1, LANE)
    ob_p, c_f, n_f, m_f = _mlstm(pad_rows(q), pad_rows(k), pad_rows(v), pad_rows(og), pad_rows(small, inert.astype(F32)),
                                 lw["gh"], st_c, st_n, m0, n_chunks=1)
    ob = ob_p.reshape(b, lpad, 512)[:, :t].reshape(m, 512)

    nr = t * AH
    nk = 8
    ql3 = qlat.reshape(b, nr, A_KVR).astype(BF16)
    qp3 = qa.reshape(b, t, AH, LANE)[..., A_NOPE:A_QK].reshape(b, nr, A_ROPE).astype(BF16)
    padk = lambda a: jnp.concatenate([a.reshape(b, t, -1), jnp.zeros((b, nk - t, a.shape[-1]), F32)], axis=1)
    ks_t = jnp.transpose(padk(ksc), (0, 2, 1))
    ks_t = jnp.tile(ks_t, (1, t, 1))
    kp_t = jnp.transpose(padk(kpe), (0, 2, 1))
    olat = _attn_paged(page_table, ql3, qp3, padk(ckv), kp_t, ks_t, *caches, layer=layer, t_new=t, n_per_step=32,
                       n_chains=4)
    olat_h = jnp.transpose(olat.reshape(b, t, AH, A_KVR), (2, 0, 1, 3)).reshape(AH, m, A_KVR)
    oc = _uv_proj(olat_h, lw["wuvp"])

    x1 = _merge(x2, g1, oa, ob, oc, gates, lw["wa"], lw["wb"], lw["wc"], lw["wo"], tm=tm, rows_per_mod=None)
    x_out = _moe(x1, sc2, sh2, g2, lw["g2"], *router, lw["wg"], lw["wu"], lw["wd"], tm=tm, rows_per_mod=None)
    state = (ckv.reshape(b, t, A_KVR), kpe.reshape(b, t, A_ROPE), ksc.reshape(b, t, AH), c_f, n_f, m_f[:, 0, :LH],
             v_sgu.reshape(b, t, SGU_W))
    return x_out, state


def kernel(x_prompt, x_sample, c_prompt, c_sample, cache_ckv, cache_kpe, cache_kscale, page_table, state_C, state_n, state_m, w_ada, b_ada, g_norm1, g_norm2, w_in, b_in, g_sgu, w_spatial, b_spatial, g_mlstm, g_cq, w_q_up, g_ckv, w_uk, w_uv, g_qn, g_kn, w_br, w_gate, b_gate, w_o, w_router, b_router, w_exp_gate, w_exp_up, w_exp_down):
    bp, tp, _ = x_prompt.shape
    bs, ts, _ = x_sample.shape
    depth = w_in.shape[0]
    past_len = page_table.shape[1] * PAGE

    mod_all = _ada(jnp.concatenate([c_prompt, c_sample], axis=0), w_ada, b_ada)
    router = _prep_router(w_router, b_router)
    tab_p = _rope_tables(jnp.arange(tp, dtype=F32))
    pos_s = jnp.tile(jnp.arange(ts, dtype=F32) + past_len, bs)
    tab_s = _rope_tables(pos_s)
    caches = (cache_ckv, jnp.swapaxes(cache_kpe, 2, 3), jnp.swapaxes(cache_kscale, 2, 3))

    xp = x_prompt.reshape(bp * tp, D)
    xs = x_sample.reshape(bs * ts, D)
    st_p, st_s = [], []
    for l in range(depth):
        lw = _prep_layer(l, w_in, b_in, w_gate, b_gate, g_norm1, g_norm2, g_sgu, g_mlstm, g_cq, w_q_up, g_ckv, w_uk,
                         w_uv, g_qn, g_kn, w_br, w_o, w_exp_gate, w_exp_up, w_exp_down)
        chunk_p = min(tp, SGU_C)
        wmix_p, bmix_p = _mix_weights(w_spatial[l], b_spatial[l], chunk_p, 1)
        wmix_s, bmix_s = _mix_weights(w_spatial[l], b_spatial[l], ts, bs)
        xp, sp = _layer_prompt(xp, mod_all[l, :bp], lw, router, tab_p, wmix_p, bmix_p, bp, tp)
        xs, ss = _layer_sample(xs, mod_all[l, bp:], lw, router, tab_s, wmix_s, bmix_s, bs, ts, state_C[l], state_n[l],
                               state_m[l], caches, page_table, l)
        st_p.append(sp)
        st_s.append(ss)
    stack = lambda sts, i: jnp.stack([s[i] for s in sts])
    return (xp.reshape(bp, tp, D), xs.reshape(bs, ts, D),
            stack(st_p, 0), stack(st_p, 1), stack(st_p, 2), stack(st_p, 3), stack(st_p, 4), stack(st_p, 5),
            stack(st_s, 0), stack(st_s, 1), stack(st_s, 2), stack(st_s, 3), stack(st_s, 4), stack(st_s, 5),
            stack(st_s, 6))
```

```python
import functools
import math

import jax
import jax.numpy as jnp
from jax import lax
from jax.experimental import pallas as pl
from jax.experimental.pallas import tpu as pltpu

F32 = jnp.float32
BF16 = jnp.bfloat16
NEG_INF = float("-inf")

EPS = 1e-6
D = 1024
SGU_G = 4
SGU_C = 128
SGU_W = SGU_G * SGU_C
LH = 4
LDK = 128
LDV = 128
LCHUNK = 128
AH = 8
A_QR = 256
A_KVR = 256
A_NOPE = 64
A_ROPE = 32
A_QK = A_NOPE + A_ROPE
A_V = 64
ROPE_BASE = 10000.0
PAGE = 128
N_EXP = 16
N_GRP = 4
GRP_E = 4
D_EXP = 256

LANE = 128
BF16_ROWS = 16
PAGED_SLOTS = 3
C_ZU, C_ZV, C_ZQ, C_ZK, C_ZVV, C_ZO = 0, 512, 1024, 1536, 2048, 2560
C_ZC = 3072
C_SMALL = 3584
C_GATE = 3712
N_CAT = C_GATE + 3 * D
SM_KR, SM_I, SM_F = 64, 96, 100

VMEM_LIMIT = 56 * 1024 * 1024


def _cparams(sem):
    return pltpu.CompilerParams(dimension_semantics=sem, vmem_limit_bytes=VMEM_LIMIT)


def _const_spec(shape):
    nd = len(shape)
    return pl.BlockSpec(shape, lambda *_: (0,) * nd, pipeline_mode=pl.Buffered(1))


def _sigmoid(x):
    return 0.5 * jnp.tanh(0.5 * x) + 0.5


def _dot(a, b):
    return jnp.dot(a, b, preferred_element_type=F32)


def _dot_nt(a, b):
    return lax.dot_general(a, b, (((1,), (1,)), ((), ())), preferred_element_type=F32)


def _rms(x, g):
    return x * lax.rsqrt(jnp.mean(x * x, axis=-1, keepdims=True) + EPS) * g


def _ada_kernel(c_ref, w_ref, b_ref, o_ref):
    c = c_ref[...]
    s = (c * _sigmoid(c)).astype(BF16)
    o_ref[...] = _dot(s, w_ref[...].astype(BF16)) + b_ref[...]


def _ada(c_all, w_ada, b_ada):
    depth, _, n = w_ada.shape
    bc = c_all.shape[0]
    tn = 1536
    return pl.pallas_call(
        _ada_kernel,
        out_shape=jax.ShapeDtypeStruct((depth, bc, n), F32),
        grid=(depth, n // tn),
        in_specs=[
            pl.BlockSpec((bc, D), lambda l, j: (0, 0)),
            pl.BlockSpec((None, D, tn), lambda l, j: (l, 0, j)),
            pl.BlockSpec((None, 1, tn), lambda l, j: (l, 0, j)),
        ],
        out_specs=pl.BlockSpec((None, bc, tn), lambda l, j: (l, 0, j)),
        compiler_params=_cparams(("arbitrary", "arbitrary")),
        name="ada",
    )(c_all, w_ada, b_ada.reshape(depth, 1, n))


def _proj_kernel(x_ref, sc_ref, sh_ref, w_ref, b_ref, g1_ref, gsgu_ref, wmix_ref, bmix_ref,
                 oa_ref, q_ref, k_ref, v_ref, og_ref, zc_ref, small_ref, gates_ref, *rest, mix_rows, emit_v):
    tm = x_ref.shape[0]
    x = x_ref[...]
    h = _rms(x, g1_ref[...]) * (1.0 + sc_ref[...]) + sh_ref[...]
    hb = h.astype(BF16)

    def sec(a, b):
        return _dot(hb, w_ref[:, a:b]) + b_ref[:, a:b]

    u = jax.nn.gelu(sec(C_ZU, C_ZU + SGU_W))
    vv = _rms(jax.nn.gelu(sec(C_ZV, C_ZV + SGU_W)), gsgu_ref[...])
    if emit_v:
        rest[0][...] = vv
    vb = vv.astype(BF16)
    for g in range(SGU_G):
        cs = slice(g * SGU_C, (g + 1) * SGU_C)
        for c in range(tm // mix_rows):
            rs = slice(c * mix_rows, (c + 1) * mix_rows)
            z = _dot(wmix_ref[g], vb[rs, cs]) + bmix_ref[:, cs]
            oa_ref[rs, cs] = (u[rs, cs] * z).astype(oa_ref.dtype)

    q_ref[...] = sec(C_ZQ, C_ZQ + 512).astype(q_ref.dtype)
    k_ref[...] = (sec(C_ZK, C_ZK + 512) * (LDK ** -0.5)).astype(k_ref.dtype)
    v_ref[...] = sec(C_ZVV, C_ZVV + 512).astype(v_ref.dtype)
    og_ref[...] = _sigmoid(sec(C_ZO, C_ZO + 512)).astype(og_ref.dtype)
    zc_ref[...] = sec(C_ZC, C_ZC + 512)
    small_ref[...] = sec(C_SMALL, C_SMALL + LANE)
    for n in range(3):
        a = C_GATE + n * D
        gates_ref[:, n * D:(n + 1) * D] = _sigmoid(sec(a, a + D)).astype(gates_ref.dtype)


def _proj(x2, sc, sh, wcat, bcat, g1, gsgu, wmix, bmix, *, tm, rows_per_mod, emit_v):
    m = x2.shape[0]
    mix_rows = wmix.shape[-1]
    if rows_per_mod is None:
        mod_spec = pl.BlockSpec((tm, D), lambda i: (i, 0))
    else:
        tpb = rows_per_mod // tm
        mod_spec = pl.BlockSpec((None, 1, D), lambda i: (i // tpb, 0, 0))
    row = lambda w: pl.BlockSpec((tm, w), lambda i: (i, 0))
    out_shape = [
        jax.ShapeDtypeStruct((m, 512), BF16),
        jax.ShapeDtypeStruct((m, 512), BF16),
        jax.ShapeDtypeStruct((m, 512), BF16),
        jax.ShapeDtypeStruct((m, 512), BF16),
        jax.ShapeDtypeStruct((m, 512), BF16),
        jax.ShapeDtypeStruct((m, 512), F32),
        jax.ShapeDtypeStruct((m, LANE), F32),
        jax.ShapeDtypeStruct((m, 3 * D), BF16),
    ]
    out_specs = [row(512)] * 5 + [row(512), row(LANE), row(3 * D)]
    if emit_v:
        out_shape.append(jax.ShapeDtypeStruct((m, SGU_W), F32))
        out_specs.append(row(SGU_W))
    return pl.pallas_call(
        functools.partial(_proj_kernel, mix_rows=mix_rows, emit_v=emit_v),
        out_shape=out_shape,
        grid=(m // tm,),
        in_specs=[
            row(D), mod_spec, mod_spec,
            _const_spec(wcat.shape), _const_spec(bcat.shape), _const_spec(g1.shape), _const_spec(gsgu.shape),
            _const_spec(wmix.shape), _const_spec(bmix.shape),
        ],
        out_specs=out_specs,
        compiler_params=_cparams(("arbitrary",)),
        name="proj",
    )(x2, sc, sh, wcat, bcat, g1, gsgu, wmix, bmix)


def _mla_kernel(zc_ref, small_ref, cos_ref, sin_ref, wq_ref, wqr_ref, gq_ref, gqr_ref, wuk_ref, wuv_ref, vone_ref,
                gcq_ref, gckv_ref, gkn_ref, gkp_ref, wukt_ref,
                ckv_ref, kpe_ref, ksc_ref, *outs, sample):
    tm = zc_ref.shape[0]
    cosv = cos_ref[...]
    sinv = sin_ref[...]
    lane = lax.broadcasted_iota(jnp.int32, (tm, LANE), 1)
    scale = A_QK ** -0.5

    cq = _rms(zc_ref[:, 0:A_QR], gcq_ref[...]).astype(BF16)
    q = _dot(cq, wq_ref[...])
    qr = _dot(cq, wqr_ref[...])

    ckv = _rms(zc_ref[:, A_QR:A_QR + A_KVR], gckv_ref[...])
    ckv_ref[...] = ckv
    ckvb = ckv.astype(BF16)
    kn = _dot(ckvb, wuk_ref[...])

    small = small_ref[...]
    zkr = jnp.where((lane >= SM_KR) & (lane < SM_KR + A_ROPE), small, 0.0)
    t = zkr * gkp_ref[...]
    half = A_ROPE // 2
    rot = jnp.where(lane < SM_KR + half, pltpu.roll(t, LANE - half, 1), pltpu.roll(t, half, 1))
    kpe = t * cosv + rot * sinv
    kpe_ref[...] = kpe[:, SM_KR:SM_KR + A_ROPE]

    ksc = jnp.zeros((tm, LANE), F32)
    if sample:
        qa_ref, ql_ref = outs
    else:
        qa_ref, ka_ref, vt_ref = outs
        vt_ref[...] = (_dot_nt(wuv_ref[...], ckvb) + vone_ref[...]).astype(vt_ref.dtype)
    for h in range(AH):
        hs = slice(h * LANE, (h + 1) * LANE)
        qb = q[:, hs]
        rinv = lax.rsqrt(jnp.sum(qb * qb, axis=-1, keepdims=True) / A_QK + EPS)
        qh = (qb * rinv * gq_ref[:, hs] * cosv + qr[:, hs] * rinv * gqr_ref[:, hs] * sinv)
        kraw = kn[:, hs] + zkr
        ks = lax.rsqrt(jnp.sum(kraw * kraw, axis=-1, keepdims=True) / A_QK + EPS)
        ksc = jnp.where(lane == h, ks, ksc)
        if sample:
            qa_ref[:, hs] = qh * scale
            qn = (qh * gkn_ref[...]).astype(BF16)
            ql_ref[:, h * A_KVR:(h + 1) * A_KVR] = _dot(qn, wukt_ref[h]) * scale
        else:
            qa_ref[:, hs] = (qh * scale).astype(qa_ref.dtype)
            ka_ref[:, hs] = ((kn[:, hs] * gkn_ref[...] + kpe) * ks).astype(ka_ref.dtype)
    ksc_ref[...] = ksc[:, 0:AH]


def _mla_prep(zc, small, cos_t, sin_t, lw, *, tm, rows_per_pos, sample):
    m = zc.shape[0]
    if rows_per_pos is None:
        pos_spec = pl.BlockSpec((tm, LANE), lambda i: (i, 0))
    else:
        tpb = rows_per_pos // tm
        pos_spec = pl.BlockSpec((tm, LANE), lambda i: (i % tpb, 0))
    row = lambda w: pl.BlockSpec((tm, w), lambda i: (i, 0))
    out_shape = [
        jax.ShapeDtypeStruct((m, A_KVR), F32),
        jax.ShapeDtypeStruct((m, A_ROPE), F32),
        jax.ShapeDtypeStruct((m, AH), F32),
    ]
    out_specs = [row(A_KVR), row(A_ROPE), row(AH)]
    if sample:
        out_shape += [jax.ShapeDtypeStruct((m, AH * LANE), F32), jax.ShapeDtypeStruct((m, AH * A_KVR), F32)]
        out_specs += [row(AH * LANE), row(AH * A_KVR)]
    else:
        n_seq = m // rows_per_pos
        out_shape += [jax.ShapeDtypeStruct((m, AH * LANE), BF16)] * 2
        out_shape += [jax.ShapeDtypeStruct((n_seq * AH * LANE, rows_per_pos), BF16)]
        out_specs += [row(AH * LANE)] * 2
        out_specs += [pl.BlockSpec((AH * LANE, tm), lambda i: (i // tpb, i % tpb))]
    names =("wq", "wqr", "gq", "gqr", "wuk", "wuv", "vone", "gcq", "gckv", "gkn", "gkp", "wukt")
    ws = [lw[n] for n in names]
    return pl.pallas_call(
        functools.partial(_mla_kernel, sample=sample),
        out_shape=out_shape,
        grid=(m // tm,),
        in_specs=[row(512), row(LANE), pos_spec, pos_spec] + [_const_spec(w.shape) for w in ws],
        out_specs=out_specs,
        compiler_params=_cparams(("arbitrary",)),
        name="mla_prep",
    )(zc, small, cos_t, sin_t, *ws)


def _split3(x):
    hi = x.astype(BF16)
    r1 = x - hi.astype(F32)
    mid = r1.astype(BF16)
    lo = (r1 - mid.astype(F32)).astype(BF16)
    return hi, mid, lo


def _mlstm_kernel(q_ref, k_ref, v_ref, og_ref, small_ref, ghb_ref, c0_ref, n0_ref, m0_ref,
                  h_ref, c_ref, n_ref, m_ref, cnt_sc):
    L = q_ref.shape[0]
    first = pl.program_id(1) == 0
    final = pl.program_id(1) == pl.num_programs(1) - 1

    @pl.when(first)
    def _():
        m_ref[...] = m0_ref[...]
        for h in range(LH):
            cnt_sc[h, 0:LDV, :] = c0_ref[h].T
            cnt_sc[h, LDV:2 * LDV, :] = jnp.broadcast_to(n0_ref[h:h + 1, :], (LDV, LDK))

    small = small_ref[...]
    s_idx = lax.broadcasted_iota(jnp.int32, (L, L), 0)
    t_idx = lax.broadcasted_iota(jnp.int32, (L, L), 1)
    upto = s_idx <= t_idx
    tril = jnp.where(t_idx <= s_idx, 1.0, 0.0).astype(BF16)
    lane = lax.broadcasted_iota(jnp.int32, (1, LANE), 1)
    ones = jnp.ones((L, LDV), BF16)
    m_all = m_ref[...]
    m_out = m_all
    for h in range(LH):
        hs = slice(h * LDK, (h + 1) * LDK)
        zf = small[:, SM_F + h:SM_F + h + 1]
        lf = jnp.broadcast_to(jnp.minimum(zf, 0.0) - jnp.log1p(jnp.exp(-jnp.abs(zf))), (L, L))
        ig = jnp.broadcast_to(small[:, SM_I + h:SM_I + h + 1], (L, L))
        b_row = jnp.sum(jnp.where(upto, lf, 0.0), axis=0, keepdims=True)
        b_bc = functools.reduce(jnp.add, [_dot(tril, piece) for piece in _split3(lf)])
        g = ig - b_bc
        logw = jnp.where(upto, b_row + g, NEG_INF)
        m_prev = m_all[:, h:h + 1]
        m_inter = m_prev + b_row
        m_t = jnp.maximum(m_inter, jnp.max(logw, axis=0, keepdims=True))
        qh = q_ref[:, hs]
        kh = k_ref[:, hs]
        v1 = jnp.concatenate([v_ref[:, hs], ones], axis=1)
        smat = _dot_nt(kh, qh) * jnp.exp(logw - m_t)
        a_inter = jnp.exp(m_inter - m_t)
        cnt = cnt_sc[h]
        nd = (lax.dot_general(v1, smat.astype(BF16), (((0,), (0,)), ((), ())), preferred_element_type=F32)
              + a_inter * _dot_nt(cnt.astype(BF16), qh))
        hv = nd[0:LDV] / jnp.maximum(jnp.abs(nd[LDV:2 * LDV]), jnp.exp(-m_t))
        hn = hv * lax.rsqrt(jnp.mean(hv * hv, axis=0, keepdims=True) + EPS) * ghb_ref[h]
        h_ref[:, hs] = (og_ref[:, hs].astype(F32) * hn.T).astype(h_ref.dtype)
        m_new = m_t[:, L - 1:L]
        b_last = b_row[:, L - 1:L]
        decay = jnp.exp(m_prev + b_last - m_new)
        wk = (kh.astype(F32) * jnp.exp(g + (b_last - m_new))).astype(BF16)
        cnt_new = decay * cnt + lax.dot_general(v1, wk, (((0,), (0,)), ((), ())), preferred_element_type=F32)
        cnt_sc[h] = cnt_new
        m_out = jnp.where(lane == h, m_new, m_out)
    m_ref[...] = m_out

    @pl.when(final)
    def _():
        for h in range(LH):
            c_ref[h] = cnt_sc[h, 0:LDV, :].T
            n_ref[h:h + 1, :] = cnt_sc[h, LDV:LDV + 1, :]


def _mlstm(q, k, v, og, small, gh, c0, n0, m0, *, n_chunks, layer):
    m = q.shape[0]
    b = c0.shape[1]
    L = m // (b * n_chunks)
    assert L == LDK == LDV, "the kernel reuses (L, L) gate matrices against (L, dk) key tiles"
    row = lambda w: pl.BlockSpec((L, w), lambda i, c: (i * n_chunks + c, 0))
    st_c = pl.BlockSpec((None, LH, LDK, LDV), lambda i, c: (i, 0, 0, 0))
    st_n = pl.BlockSpec((None, LH, LDK), lambda i, c: (i, 0, 0))
    st_m = pl.BlockSpec((None, 1, LANE), lambda i, c: (i, 0, 0))
    in_c = pl.BlockSpec((None, None, LH, LDK, LDV), lambda i, c: (layer, i, 0, 0, 0))
    in_n = pl.BlockSpec((None, None, LH, LDK), lambda i, c: (layer, i, 0, 0))
    return pl.pallas_call(
        _mlstm_kernel,
        out_shape=[
            jax.ShapeDtypeStruct((m, 512), BF16),
            jax.ShapeDtypeStruct((b, LH, LDK, LDV), F32),
            jax.ShapeDtypeStruct((b, LH, LDK), F32),
            jax.ShapeDtypeStruct((b, 1, LANE), F32),
        ],
        grid=(b, n_chunks),
        in_specs=[row(512), row(512), row(512), row(512), row(LANE), _const_spec(gh.shape), in_c, in_n, st_m],
        out_specs=[row(512), st_c, st_n, st_m],
        scratch_shapes=[pltpu.VMEM((LH, 2 * LDV, LDK), F32)],
        compiler_params=_cparams(("arbitrary", "arbitrary")),
        name="mlstm",
    )(q, k, v, og, small, gh, c0, n0, m0)


def _attn_kernel(q_ref, k_ref, vt_ref, o_ref, p_sc, *, tq):
    t = q_ref.shape[0]
    key = lax.broadcasted_iota(jnp.int32, (tq, tq), 0)
    qry = lax.broadcasted_iota(jnp.int32, (tq, tq), 1)
    for i in range(t // tq):
        r0, r1 = i * tq, (i + 1) * tq
        st = _dot_nt(k_ref[0:r1, :], q_ref[r0:r1, :])
        st_d = jnp.where(key <= qry, st[r0:], NEG_INF)
        m = jnp.max(st_d, axis=0, keepdims=True)
        if i > 0:
            m = jnp.maximum(m, jnp.max(st[:r0], axis=0, keepdims=True))
            p_sc[0:r0] = jnp.exp(st[:r0] - m).astype(BF16)
        p_sc[r0:r1] = jnp.exp(st_d - m).astype(BF16)
        ot = _dot(vt_ref[:, 0:r1], p_sc[0:r1])
        o_ref[r0:r1, :] = (ot / ot[A_V:A_V + 1, :]).T.astype(o_ref.dtype)


def _attn_prompt(qa, ka, vt, *, b, t, tq):
    blk = pl.BlockSpec((t, LANE), lambda i, h: (i, h))
    return pl.pallas_call(
        functools.partial(_attn_kernel, tq=tq),
        out_shape=jax.ShapeDtypeStruct(qa.shape, BF16),
        grid=(b, AH),
        in_specs=[blk, blk, pl.BlockSpec((LANE, t), lambda i, h: (i * AH + h, 0))],
        out_specs=blk,
        scratch_shapes=[pltpu.VMEM((t, tq), BF16)],
        compiler_params=_cparams(("arbitrary", "arbitrary")),
        name="attn_prompt",
    )(qa, ka, vt)


def _paged_kernel(pt_ref, ql_ref, qp_ref, cn_ref, kpn_ref, ksn_ref, ckv_hbm, kpe_hbm, ks_hbm,
                  o_ref, cbuf, kpbuf, ksbuf, sems, m_sc, l_sc, acc_sc, *, layer, n_per_step, n_chains, t_new):
    j = pl.program_id(1)
    steps = pl.num_programs(1)
    step = pl.program_id(0) * steps + j
    last = pl.num_programs(0) * steps - 1
    slot = step % PAGED_SLOTS
    ql = ql_ref[...]
    qp = qp_ref[...]
    nr = ql.shape[0]

    def page_copies(step_, n):
        slot_ = step_ % PAGED_SLOTS
        pg = pt_ref[jnp.minimum(step_, last) * n_per_step + n]
        return (pltpu.make_async_copy(ckv_hbm.at[layer, pg], cbuf.at[slot_, n], sems.at[slot_, 0]),
                pltpu.make_async_copy(kpe_hbm.at[layer, pg], kpbuf.at[slot_, n], sems.at[slot_, 1]),
                pltpu.make_async_copy(ks_hbm.at[layer, pg], ksbuf.at[slot_, n], sems.at[slot_, 2]))

    @pl.when(step == 0)
    def _():
        for ahead in range(PAGED_SLOTS - 1):
            for n in range(n_per_step):
                for cp in page_copies(ahead, n):
                    cp.start()

    for n in range(n_per_step):
        for cp in page_copies(step, n):
            cp.wait()

    @pl.when(j == 0)
    def _():
        cn = cn_ref[...].astype(BF16)
        s = (_dot_nt(ql, cn) + _dot(qp, kpn_ref[...].astype(BF16))) * ksn_ref[...]
        nk = cn.shape[0]
        r_t = lax.broadcasted_iota(jnp.int32, (nr, nk), 0) // AH
        key = lax.broadcasted_iota(jnp.int32, (nr, nk), 1)
        s = jnp.where((key <= r_t) & (key < t_new), s, NEG_INF)
        m0 = jnp.max(s, axis=-1, keepdims=True)
        p = jnp.exp(s - m0)
        m_sc[0] = m0
        l_sc[0] = jnp.sum(p, axis=-1, keepdims=True)
        acc_sc[0] = _dot(p.astype(BF16), cn)
        for c in range(1, n_chains):
            m_sc[c] = m0
            l_sc[c] = jnp.zeros_like(m0)
            acc_sc[c] = jnp.zeros((nr, A_KVR), F32)

    per_chain = n_per_step // n_chains
    cbs, ss = [], []
    for n in range(n_per_step):
        for cp in page_copies(step + PAGED_SLOTS - 1, n):
            cp.start()
        cb = cbuf[slot, n].astype(BF16)
        kpt = kpbuf[slot, n].astype(BF16)
        ks = ksbuf[slot, n]
        kst = jnp.concatenate([ks] * t_new, axis=0)
        cbs.append(cb)
        ss.append((_dot_nt(ql, cb) + _dot(qp, kpt)) * kst)
    finals = []
    for c in range(n_chains):
        sc = ss[c * per_chain:(c + 1) * per_chain]
        cc = cbs[c * per_chain:(c + 1) * per_chain]
        m_i = m_sc[c]
        m_new = jnp.maximum(m_i, jnp.max(functools.reduce(jnp.maximum, sc), axis=-1, keepdims=True))
        alpha = jnp.exp(m_i - m_new)
        ps = [jnp.exp(s - m_new) for s in sc]
        pv = functools.reduce(jnp.add, [_dot(p.astype(BF16), cb) for p, cb in zip(ps, cc)])
        l_i = alpha * l_sc[c] + jnp.sum(functools.reduce(jnp.add, ps), axis=-1, keepdims=True)
        acc = alpha * acc_sc[c] + pv
        m_sc[c] = m_new
        l_sc[c] = l_i
        acc_sc[c] = acc
        finals.append((m_new, l_i, acc))

    @pl.when(j == steps - 1)
    def _():
        m_f = functools.reduce(jnp.maximum, [f[0] for f in finals])
        ws = [jnp.exp(f[0] - m_f) for f in finals]
        l_f = functools.reduce(jnp.add, [w * f[1] for w, f in zip(ws, finals)])
        acc_f = functools.reduce(jnp.add, [w * f[2] for w, f in zip(ws, finals)])
        o_ref[...] = acc_f / l_f

    @pl.when(step == last)
    def _():
        for ahead in range(1, PAGED_SLOTS):
            for n in range(n_per_step):
                for cp in page_copies(last + ahead, n):
                    cp.wait()


def _attn_paged(page_table, qlat, qp, c_new, kp_new_t, ks_new_t, cache_ckv, cache_kpe_t, cache_ks_t, *, layer, t_new,
                n_per_step, n_chains):
    bs, nr, _ = qlat.shape
    n_pages = page_table.shape[1]
    nk = c_new.shape[1]
    steps = n_pages // n_per_step
    pt_flat = page_table.reshape(-1)

    per_b = lambda s1, s2: pl.BlockSpec((None, s1, s2), lambda b, j, pt: (b, 0, 0))
    hbm = pl.BlockSpec(memory_space=pl.ANY)
    grid_spec = pltpu.PrefetchScalarGridSpec(
        num_scalar_prefetch=1,
        grid=(bs, steps),
        in_specs=[per_b(nr, A_KVR), per_b(nr, A_ROPE), per_b(nk, A_KVR), per_b(A_ROPE, nk), per_b(nr, nk),
                  hbm, hbm, hbm],
        out_specs=per_b(nr, A_KVR),
        scratch_shapes=[
            pltpu.VMEM((PAGED_SLOTS, n_per_step, PAGE, A_KVR), F32),
            pltpu.VMEM((PAGED_SLOTS, n_per_step, A_ROPE, PAGE), F32),
            pltpu.VMEM((PAGED_SLOTS, n_per_step, AH, PAGE), F32),
            pltpu.SemaphoreType.DMA((PAGED_SLOTS, 3)),
            pltpu.VMEM((n_chains, nr, 1), F32), pltpu.VMEM((n_chains, nr, 1), F32),
            pltpu.VMEM((n_chains, nr, A_KVR), F32),
        ],
    )
    return pl.pallas_call(
        functools.partial(_paged_kernel, layer=layer, n_per_step=n_per_step, n_chains=n_chains, t_new=t_new),
        out_shape=jax.ShapeDtypeStruct((bs, nr, A_KVR), F32),
        grid_spec=grid_spec,
        compiler_params=_cparams(("arbitrary", "arbitrary")),
        name="attn_paged",
    )(pt_flat, qlat, qp, c_new, kp_new_t, ks_new_t, cache_ckv, cache_kpe_t, cache_ks_t)


def _uv_kernel(x_ref, w_ref, o_ref):
    o_ref[...] = _dot(x_ref[...].astype(BF16), w_ref[...]).astype(o_ref.dtype)


def _uv_proj(olat_h, wuvp):
    _, m, _ = olat_h.shape
    return pl.pallas_call(
        _uv_kernel,
        out_shape=jax.ShapeDtypeStruct((m, AH * LANE), BF16),
        grid=(AH,),
        in_specs=[pl.BlockSpec((None, m, A_KVR), lambda h: (h, 0, 0)),
                  pl.BlockSpec((None, A_KVR, LANE), lambda h: (h, 0, 0))],
        out_specs=pl.BlockSpec((m, LANE), lambda h: (0, h)),
        compiler_params=_cparams(("arbitrary",)),
        name="uv_proj",
    )(olat_h, wuvp)


def _merge_kernel(x_ref, g1_ref, oa_ref, ob_ref, oc_ref, gates_ref, wa_ref, wb_ref, wc_ref, wo_ref, o_ref):
    mix = gates_ref[:, 0:D].astype(F32) * _dot(oa_ref[...], wa_ref[...])
    mix += gates_ref[:, D:2 * D].astype(F32) * _dot(ob_ref[...], wb_ref[...])
    mix += gates_ref[:, 2 * D:3 * D].astype(F32) * _dot(oc_ref[...], wc_ref[...])
    o_ref[...] = x_ref[...] + g1_ref[...] * _dot(mix.astype(BF16), wo_ref[...])


def _mod_spec(tm, rows_per_mod):
    if rows_per_mod is None:
        return pl.BlockSpec((tm, D), lambda i: (i, 0))
    tpb = rows_per_mod // tm
    return pl.BlockSpec((None, 1, D), lambda i: (i // tpb, 0, 0))


def _merge(x2, g1, oa, ob, oc, gates, wa, wb, wc, wo, *, tm, rows_per_mod):
    m = x2.shape[0]
    row = lambda w: pl.BlockSpec((tm, w), lambda i: (i, 0))
    return pl.pallas_call(
        _merge_kernel,
        out_shape=jax.ShapeDtypeStruct((m, D), F32),
        grid=(m // tm,),
        in_specs=[row(D), _mod_spec(tm, rows_per_mod), row(512), row(512), row(AH * LANE), row(3 * D),
                  _const_spec(wa.shape), _const_spec(wb.shape), _const_spec(wc.shape), _const_spec(wo.shape)],
        out_specs=row(D),
        compiler_params=_cparams(("arbitrary",)),
        name="merge",
    )(x2, g1, oa, ob, oc, gates, wa, wb, wc, wo)


def _moe_kernel(x_ref, sc_ref, sh_ref, g2_ref, gn_ref, wr1_ref, wr2_ref, br_ref, wg_ref, wu_ref, wd_ref, o_ref):
    tm = x_ref.shape[0]
    x = x_ref[...]
    h2 = _rms(x, gn_ref[...]) * (1.0 + sc_ref[...]) + sh_ref[...]
    hi = h2.astype(BF16)
    lo = (h2 - hi.astype(F32)).astype(BF16)
    r = _dot(hi, wr1_ref[...]) + _dot(lo, wr2_ref[...])
    rt = r.T
    logits = rt[0:32] + rt[32:64]
    scores = _sigmoid(logits)
    sel = scores + br_ref[...]
    e = [sel[8 * p:8 * p + 8] for p in range(GRP_E)]
    sc = [scores[8 * p:8 * p + 8] for p in range(GRP_E)]
    a, b = jnp.maximum(e[0], e[1]), jnp.minimum(e[0], e[1])
    c, d = jnp.maximum(e[2], e[3]), jnp.minimum(e[2], e[3])
    top1 = jnp.maximum(a, c)
    top2 = jnp.maximum(jnp.minimum(a, c), jnp.maximum(b, d))
    rowi = lax.broadcasted_iota(jnp.int32, (8, tm), 0).astype(F32)
    gs = jnp.where(rowi < N_GRP, top1 + top2, NEG_INF)
    gmax = jnp.max(gs, axis=0, keepdims=True)
    gidx = jnp.min(jnp.where(gs == gmax, rowi, 8.0), axis=0, keepdims=True)
    oh = rowi == gidx
    s = [jnp.sum(jnp.where(oh, e[p], 0.0), axis=0, keepdims=True) for p in range(GRP_E)]
    w = [jnp.sum(jnp.where(oh, sc[p], 0.0), axis=0, keepdims=True) for p in range(GRP_E)]

    def first_argmax(vals):
        best = jnp.maximum(jnp.maximum(vals[0], vals[1]), jnp.maximum(vals[2], vals[3]))
        return jnp.where(vals[0] == best, 0, jnp.where(vals[1] == best, 1, jnp.where(vals[2] == best, 2, 3)))

    i1 = first_argmax(s)
    i2 = first_argmax([jnp.where(i1 == p, NEG_INF, s[p]) for p in range(GRP_E)])
    w1 = sum(jnp.where(i1 == p, w[p], 0.0) for p in range(GRP_E))
    w2 = sum(jnp.where(i2 == p, w[p], 0.0) for p in range(GRP_E))
    den = w1 + w2
    w1 = w1 / den
    w2 = w2 / den
    comb = [jnp.where(oh, jnp.where(i1 == p, w1, 0.0) + jnp.where(i2 == p, w2, 0.0), 0.0) for p in range(GRP_E)]
    comb_t = jnp.concatenate(comb + [jnp.zeros((LANE - 8 * GRP_E, tm), F32)], axis=0)
    combine = comb_t.T

    acc = jnp.zeros((tm, D), F32)
    for g in range(N_GRP):
        acts = []
        for p in range(GRP_E):
            ex = g * GRP_E + p
            hg = _dot(hi, wg_ref[ex])
            hu = _dot(hi, wu_ref[ex])
            cw = combine[:, 8 * p + g:8 * p + g + 1]
            acts.append((hg * _sigmoid(hg) * hu * cw).astype(BF16))
        acc += _dot(jnp.concatenate(acts, axis=1), wd_ref[g])
    o_ref[...] = x + g2_ref[...] * acc


def _moe(x2, sc, sh, g2, gn, wr1, wr2, br, wg, wu, wd, *, tm, rows_per_mod):
    m = x2.shape[0]
    row = lambda w: pl.BlockSpec((tm, w), lambda i: (i, 0))
    ms = _mod_spec(tm, rows_per_mod)
    return pl.pallas_call(
        _moe_kernel,
        out_shape=jax.ShapeDtypeStruct((m, D), F32),
        grid=(m // tm,),
        in_specs=[row(D), ms, ms, ms, _const_spec(gn.shape), _const_spec(wr1.shape), _const_spec(wr2.shape),
                  _const_spec(br.shape), _const_spec(wg.shape), _const_spec(wu.shape), _const_spec(wd.shape)],
        out_specs=row(D),
        compiler_params=_cparams(("arbitrary",)),
        name="moe",
    )(x2, sc, sh, g2, gn, wr1, wr2, br, wg, wu, wd)


def _head_block(nope, r1, r2):
    pad = jnp.zeros(nope.shape[:-1] + (LANE - A_QK,), nope.dtype)
    blk = jnp.concatenate([nope, r1, r2, pad], axis=-1)
    return blk.reshape(blk.shape[:-2] + (AH * LANE,))


def _prep_layer(l, w_in, b_in, w_gate, b_gate, g_norm1, g_norm2, g_sgu, g_mlstm, g_cq, w_q_up, g_ckv, w_uk, w_uv,
                g_qn, g_kn, w_br, w_o, w_exp_gate, w_exp_up, w_exp_down):
    half = A_ROPE // 2
    wi, bi = w_in[l], b_in[l]

    def cat_cols(a, gate):
        z = lambda n: jnp.zeros(a.shape[:-1] + (n,), a.dtype)
        small = jnp.concatenate([z(SM_KR), a[..., 3592:3624], a[..., 3072:3080], z(LANE - SM_F - LH)], axis=-1)
        return jnp.concatenate([a[..., :3072], a[..., 3080:3592], small, gate], axis=-1)

    lw = {}
    lw["wcat"] = cat_cols(wi, w_gate[l]).astype(BF16)
    lw["bcat"] = cat_cols(bi, b_gate[l]).reshape(1, N_CAT)
    lw["g1"] = g_norm1[l].reshape(1, D)
    lw["g2"] = g_norm2[l].reshape(1, D)
    lw["gsgu"] = g_sgu[l].reshape(1, SGU_W)
    lw["gh"] = jnp.broadcast_to(g_mlstm[l][:, :, None], (LH, LDV, LCHUNK))
    wq = w_q_up[l]
    nope, r1, r2 = wq[..., :A_NOPE], wq[..., A_NOPE:A_NOPE + half], wq[..., A_NOPE + half:]
    lw["wq"] = _head_block(nope, r1, r2).astype(BF16)
    lw["wqr"] = _head_block(jnp.zeros_like(nope), r2, r1).astype(BF16)
    gq = jnp.broadcast_to(g_qn[l], (AH, A_QK))
    gn_, g1_, g2_ = gq[:, :A_NOPE], gq[:, A_NOPE:A_NOPE + half], gq[:, A_NOPE + half:]
    lw["gq"] = _head_block(gn_, g1_, g2_).reshape(1, AH * LANE)
    lw["gqr"] = _head_block(jnp.zeros_like(gn_), g2_, g1_).reshape(1, AH * LANE)
    zr = jnp.zeros((A_KVR, AH, half), F32)
    lw["wuk"] = _head_block(w_uk[l], zr, zr).astype(BF16)
    uv = jnp.concatenate([w_uv[l], jnp.zeros((A_KVR, AH, LANE - A_V), F32)], axis=-1)
    lw["wuv"] = uv.reshape(A_KVR, AH * LANE).T.astype(BF16)
    lw["wuvp"] = jnp.transpose(uv, (1, 0, 2)).astype(BF16)
    lw["vone"] = jnp.tile((jnp.arange(LANE) == A_V).astype(F32), AH).reshape(AH * LANE, 1)
    lw["gcq"] = g_cq[l].reshape(1, A_QR)
    lw["gckv"] = g_ckv[l].reshape(1, A_KVR)
    gk = g_kn[l]
    lw["gkn"] = jnp.concatenate([gk[:A_NOPE], jnp.zeros((LANE - A_NOPE,), F32)]).reshape(1, LANE)
    lw["gkp"] = jnp.concatenate([jnp.zeros((SM_KR,), F32), gk[A_NOPE:], jnp.zeros((LANE - A_QK,), F32)]).reshape(1, LANE)
    ukt = jnp.transpose(w_uk[l], (1, 2, 0))
    lw["wukt"] = jnp.concatenate([ukt, jnp.zeros((AH, LANE - A_NOPE, A_KVR), F32)], axis=1).astype(BF16)
    lw["wa"] = w_br[l, 0].astype(BF16)
    lw["wb"] = w_br[l, 1].astype(BF16)
    wc = w_br[l, 2].reshape(AH, A_V, D)
    lw["wc"] = jnp.concatenate([wc, jnp.zeros((AH, LANE - A_V, D), F32)], axis=1).reshape(AH * LANE, D).astype(BF16)
    lw["wo"] = w_o[l].astype(BF16)
    lw["wg"] = w_exp_gate[l].astype(BF16)
    lw["wu"] = w_exp_up[l].astype(BF16)
    lw["wd"] = w_exp_down[l].reshape(N_GRP, GRP_E * D_EXP, D).astype(BF16)
    return lw


def _prep_router(w_router, b_router):
    wr = w_router.reshape(D, N_GRP, GRP_E).transpose(0, 2, 1)
    wr = jnp.concatenate([wr, jnp.zeros((D, GRP_E, 8 - N_GRP), F32)], axis=-1).reshape(D, 8 * GRP_E)
    hi = wr.astype(BF16)
    lo = (wr - hi.astype(F32)).astype(BF16)
    z = lambda n: jnp.zeros((D, n), BF16)
    wr1 = jnp.concatenate([hi, lo, z(LANE - 64)], axis=1)
    wr2 = jnp.concatenate([hi, z(LANE - 32)], axis=1)
    br = b_router.reshape(N_GRP, GRP_E).T
    br = jnp.concatenate([br, jnp.zeros((GRP_E, 8 - N_GRP), F32)], axis=-1).reshape(8 * GRP_E, 1)
    return wr1, wr2, br


def _rope_tables(pos):
    half = A_ROPE // 2
    freqs = ROPE_BASE ** (-jnp.arange(half, dtype=F32) / half)
    ang = pos[:, None] * freqs[None, :]
    cos, sin = jnp.cos(ang), jnp.sin(ang)
    n = pos.shape[0]
    cos_t = jnp.concatenate([jnp.ones((n, A_NOPE), F32), cos, cos, jnp.zeros((n, LANE - A_QK), F32)], axis=1)
    sin_t = jnp.concatenate([jnp.zeros((n, A_NOPE), F32), -sin, sin, jnp.zeros((n, LANE - A_QK), F32)], axis=1)
    return cos_t, sin_t


def _mix_weights(w_spatial_l, b_spatial_l, chunk, reps):
    w = jnp.where(jnp.tril(jnp.ones((chunk, chunk), dtype=bool))[None], w_spatial_l[:, :chunk, :chunk], 0.0)
    if reps > 1:
        idx = jnp.arange(reps * chunk)
        pos = jax.nn.one_hot(idx % chunk, chunk, dtype=F32)
        same = (idx[:, None] // chunk) == (idx[None, :] // chunk)
        w = jnp.where(same[None], jnp.einsum("rt,gts,cs->grc", pos, w, pos, precision=lax.Precision.HIGHEST), 0.0)
    bias = jnp.tile(b_spatial_l[:, :chunk].T, (reps, 1))
    bias = jnp.repeat(bias, SGU_C, axis=1)
    return w.astype(BF16), bias


def _layer_prompt(x2, mod, lw, router, tables, wmix, bmix, b, t):
    m = b * t
    tm = 512
    sh1, sc1, g1, sh2, sc2, g2 = [a.reshape(b, 1, D) for a in jnp.split(mod, 6, axis=-1)]
    oa, q, k, v, og, zc, small, gates = _proj(x2, sc1, sh1, lw["wcat"], lw["bcat"], lw["g1"], lw["gsgu"], wmix, bmix,
                                              tm=tm, rows_per_mod=t, emit_v=False)
    ckv, kpe, ksc, qa, ka, va = _mla_prep(zc, small, tables[0], tables[1], lw, tm=tm, rows_per_pos=t, sample=False)
    n_chunks = t // LCHUNK
    c0 = jnp.zeros((b, LH, LDK, LDV), F32)
    n0 = jnp.zeros((b, LH, LDK), F32)
    m0 = jnp.zeros((b, 1, LANE), F32)
    ob, c_f, n_f, m_f = _mlstm(q, k, v, og, small, lw["gh"], c0[None], n0[None], m0, n_chunks=n_chunks, layer=0)
    oc = _attn_prompt(qa, ka, va, b=b, t=t, tq=min(t, 256))
    x1 = _merge(x2, g1, oa, ob, oc, gates, lw["wa"], lw["wb"], lw["wc"], lw["wo"], tm=tm, rows_per_mod=t)
    x_out = _moe(x1, sc2, sh2, g2, lw["g2"], *router, lw["wg"], lw["wu"], lw["wd"], tm=tm, rows_per_mod=t)
    state = (ckv.reshape(b, t, A_KVR), kpe.reshape(b, t, A_ROPE), ksc.reshape(b, t, AH), c_f, n_f, m_f[:, 0, :LH])
    return x_out, state


def _layer_sample(x2, mod, lw, router, tables, wmix, bmix, b, t, st_c, st_n, st_m, caches, page_table, layer):
    m = b * t
    tm = m
    mods = [jnp.repeat(a, t, axis=0) for a in jnp.split(mod, 6, axis=-1)]
    sh1, sc1, g1, sh2, sc2, g2 = mods
    oa, q, k, v, og, zc, small, gates, v_sgu = _proj(x2, sc1, sh1, lw["wcat"], lw["bcat"], lw["g1"], lw["gsgu"], wmix,
                                                     bmix, tm=tm, rows_per_mod=None, emit_v=True)
    ckv, kpe, ksc, qa, qlat = _mla_prep(zc, small, tables[0], tables[1], lw, tm=tm, rows_per_pos=None, sample=True)

    lpad = LCHUNK

    def pad_rows(a, fill=None):
        a3 = a.reshape(b, t, a.shape[-1])
        if fill is None:
            padv = jnp.zeros((b, lpad - t, a.shape[-1]), a.dtype)
        else:
            padv = jnp.broadcast_to(fill, (b, lpad - t, a.shape[-1])).astype(a.dtype)
        return jnp.concatenate([a3, padv], axis=1).reshape(b * lpad, a.shape[-1])

    lane = jnp.arange(LANE)
    inert = jnp.where((lane >= SM_I) & (lane < SM_I + LH), -1e30, jnp.where((lane >= SM_F) & (lane < SM_F + LH), 1e30, 0.0))
    m0 = jnp.concatenate([st_m, jnp.zeros((b, LANE - LH), F32)], axis=1).reshape(b, 1, LANE)
    ob_p, c_f, n_f, m_f = _mlstm(pad_rows(q), pad_rows(k), pad_rows(v), pad_rows(og), pad_rows(small, inert.astype(F32)),
                                 lw["gh"], st_c, st_n, m0, n_chunks=1, layer=layer)
    ob = ob_p.reshape(b, lpad, 512)[:, :t].reshape(m, 512)

    nr = t * AH
    nk = 8
    ql3 = qlat.reshape(b, nr, A_KVR).astype(BF16)
    qp3 = qa.reshape(b, t, AH, LANE)[..., A_NOPE:A_QK].reshape(b, nr, A_ROPE).astype(BF16)
    padk = lambda a: jnp.concatenate([a.reshape(b, t, -1), jnp.zeros((b, nk - t, a.shape[-1]), F32)], axis=1)
    ks_t = jnp.transpose(padk(ksc), (0, 2, 1))
    ks_t = jnp.tile(ks_t, (1, t, 1))
    kp_t = jnp.transpose(padk(kpe), (0, 2, 1))
    olat = _attn_paged(page_table, ql3, qp3, padk(ckv), kp_t, ks_t, *caches, layer=layer, t_new=t, n_per_step=32,
                       n_chains=4)
    olat_h = jnp.transpose(olat.reshape(b, t, AH, A_KVR), (2, 0, 1, 3)).reshape(AH, m, A_KVR)
    oc = _uv_proj(olat_h, lw["wuvp"])

    x1 = _merge(x2, g1, oa, ob, oc, gates, lw["wa"], lw["wb"], lw["wc"], lw["wo"], tm=tm, rows_per_mod=None)
    x_out = _moe(x1, sc2, sh2, g2, lw["g2"], *router, lw["wg"], lw["wu"], lw["wd"], tm=tm, rows_per_mod=None)
    state = (ckv.reshape(b, t, A_KVR), kpe.reshape(b, t, A_ROPE), ksc.reshape(b, t, AH), c_f, n_f, m_f[:, 0, :LH],
             v_sgu.reshape(b, t, SGU_W))
    return x_out, state


def kernel(x_prompt, x_sample, c_prompt, c_sample, cache_ckv, cache_kpe, cache_kscale, page_table, state_C, state_n, state_m, w_ada, b_ada, g_norm1, g_norm2, w_in, b_in, g_sgu, w_spatial, b_spatial, g_mlstm, g_cq, w_q_up, g_ckv, w_uk, w_uv, g_qn, g_kn, w_br, w_gate, b_gate, w_o, w_router, b_router, w_exp_gate, w_exp_up, w_exp_down):
    bp, tp, _ = x_prompt.shape
    bs, ts, _ = x_sample.shape
    depth = w_in.shape[0]
    past_len = page_table.shape[1] * PAGE

    mod_all = _ada(jnp.concatenate([c_prompt, c_sample], axis=0), w_ada, b_ada)
    router = _prep_router(w_router, b_router)
    tab_p = _rope_tables(jnp.arange(tp, dtype=F32))
    pos_s = jnp.tile(jnp.arange(ts, dtype=F32) + past_len, bs)
    tab_s = _rope_tables(pos_s)
    caches = (cache_ckv, jnp.swapaxes(cache_kpe, 2, 3), jnp.swapaxes(cache_kscale, 2, 3))

    xp = x_prompt.reshape(bp * tp, D)
    xs = x_sample.reshape(bs * ts, D)
    st_p, st_s = [], []
    for l in range(depth):
        lw = _prep_layer(l, w_in, b_in, w_gate, b_gate, g_norm1, g_norm2, g_sgu, g_mlstm, g_cq, w_q_up, g_ckv, w_uk,
                         w_uv, g_qn, g_kn, w_br, w_o, w_exp_gate, w_exp_up, w_exp_down)
        chunk_p = min(tp, SGU_C)
        wmix_p, bmix_p = _mix_weights(w_spatial[l], b_spatial[l], chunk_p, 1)
        wmix_s, bmix_s = _mix_weights(w_spatial[l], b_spatial[l], ts, bs)
        xp, sp = _layer_prompt(xp, mod_all[l, :bp], lw, router, tab_p, wmix_p, bmix_p, bp, tp)
        xs, ss = _layer_sample(xs, mod_all[l, bp:], lw, router, tab_s, wmix_s, bmix_s, bs, ts, state_C, state_n,
                               state_m[l], caches, page_table, l)
        st_p.append(sp)
        st_s.append(ss)
    stack = lambda sts, i: jnp.stack([s[i] for s in sts])
    return (xp.reshape(bp, tp, D), xs.reshape(bs, ts, D),
            stack(st_p, 0), stack(st_p, 1), stack(st_p, 2), stack(st_p, 3), stack(st_p, 4), stack(st_p, 5),
            stack(st_s, 0), stack(st_s, 1), stack(st_s, 2), stack(st_s, 3), stack(st_s, 4), stack(st_s, 5),
            stack(st_s, 6))
```

```python
import functools
import math

import jax
import jax.numpy as jnp
from jax import lax
from jax.experimental import pallas as pl
from jax.experimental.pallas import tpu as pltpu

F32 = jnp.float32
BF16 = jnp.bfloat16
NEG_INF = float("-inf")

EPS = 1e-6
D = 1024
SGU_G = 4
SGU_C = 128
SGU_W = SGU_G * SGU_C
LH = 4
LDK = 128
LDV = 128
LCHUNK = 128
AH = 8
A_QR = 256
A_KVR = 256
A_NOPE = 64
A_ROPE = 32
A_QK = A_NOPE + A_ROPE
A_V = 64
ROPE_BASE = 10000.0
PAGE = 128
N_EXP = 16
N_GRP = 4
GRP_E = 4
D_EXP = 256

LANE = 128
BF16_ROWS = 16
PAGED_SLOTS = 3
C_ZU, C_ZV, C_ZQ, C_ZK, C_ZVV, C_ZO = 0, 512, 1024, 1536, 2048, 2560
C_ZC = 3072
C_SMALL = 3584
C_GATE = 3712
N_CAT = C_GATE + 3 * D
SM_KR, SM_I, SM_F = 64, 96, 100

VMEM_LIMIT = 56 * 1024 * 1024


def _cparams(sem):
    return pltpu.CompilerParams(dimension_semantics=sem, vmem_limit_bytes=VMEM_LIMIT)


def _const_spec(shape):
    nd = len(shape)
    return pl.BlockSpec(shape, lambda *_: (0,) * nd, pipeline_mode=pl.Buffered(1))


def _sigmoid(x):
    return 0.5 * jnp.tanh(0.5 * x) + 0.5


def _dot(a, b):
    return jnp.dot(a, b, preferred_element_type=F32)


def _dot_nt(a, b):
    return lax.dot_general(a, b, (((1,), (1,)), ((), ())), preferred_element_type=F32)


def _rms(x, g):
    return x * lax.rsqrt(jnp.mean(x * x, axis=-1, keepdims=True) + EPS) * g


def _ada_kernel(c_ref, w_ref, b_ref, o_ref):
    c = c_ref[...]
    s = (c * _sigmoid(c)).astype(BF16)
    o_ref[...] = _dot(s, w_ref[...].astype(BF16)) + b_ref[...]


def _ada(c_all, w_ada, b_ada):
    depth, _, n = w_ada.shape
    bc = c_all.shape[0]
    tn = 1536
    return pl.pallas_call(
        _ada_kernel,
        out_shape=jax.ShapeDtypeStruct((depth, bc, n), F32),
        grid=(depth, n // tn),
        in_specs=[
            pl.BlockSpec((bc, D), lambda l, j: (0, 0)),
            pl.BlockSpec((None, D, tn), lambda l, j: (l, 0, j)),
            pl.BlockSpec((None, 1, tn), lambda l, j: (l, 0, j)),
        ],
        out_specs=pl.BlockSpec((None, bc, tn), lambda l, j: (l, 0, j)),
        compiler_params=_cparams(("arbitrary", "arbitrary")),
        name="ada",
    )(c_all, w_ada, b_ada.reshape(depth, 1, n))


def _proj_kernel(x_ref, sc_ref, sh_ref, w_ref, b_ref, g1_ref, gsgu_ref, wmix_ref, bmix_ref,
                 oa_ref, q_ref, k_ref, v_ref, og_ref, zc_ref, small_ref, gates_ref, *rest, mix_rows, emit_v):
    tm = x_ref.shape[0]
    x = x_ref[...]
    h = _rms(x, g1_ref[...]) * (1.0 + sc_ref[...]) + sh_ref[...]
    hb = h.astype(BF16)

    def sec(a, b):
        return _dot(hb, w_ref[:, a:b]) + b_ref[:, a:b]

    u = jax.nn.gelu(sec(C_ZU, C_ZU + SGU_W))
    vv = _rms(jax.nn.gelu(sec(C_ZV, C_ZV + SGU_W)), gsgu_ref[...])
    if emit_v:
        rest[0][...] = vv
    vb = vv.astype(BF16)
    for g in range(SGU_G):
        cs = slice(g * SGU_C, (g + 1) * SGU_C)
        for c in range(tm // mix_rows):
            rs = slice(c * mix_rows, (c + 1) * mix_rows)
            z = _dot(wmix_ref[g], vb[rs, cs]) + bmix_ref[:, cs]
            oa_ref[rs, cs] = (u[rs, cs] * z).astype(oa_ref.dtype)

    q_ref[...] = sec(C_ZQ, C_ZQ + 512).astype(q_ref.dtype)
    k_ref[...] = (sec(C_ZK, C_ZK + 512) * (LDK ** -0.5)).astype(k_ref.dtype)
    v_ref[...] = sec(C_ZVV, C_ZVV + 512).astype(v_ref.dtype)
    og_ref[...] = _sigmoid(sec(C_ZO, C_ZO + 512)).astype(og_ref.dtype)
    zc_ref[...] = sec(C_ZC, C_ZC + 512)
    small_ref[...] = sec(C_SMALL, C_SMALL + LANE)
    for n in range(3):
        a = C_GATE + n * D
        gates_ref[:, n * D:(n + 1) * D] = _sigmoid(sec(a, a + D)).astype(gates_ref.dtype)


def _proj(x2, sc, sh, wcat, bcat, g1, gsgu, wmix, bmix, *, tm, rows_per_mod, emit_v):
    m = x2.shape[0]
    mix_rows = wmix.shape[-1]
    if rows_per_mod is None:
        mod_spec = pl.BlockSpec((tm, D), lambda i: (i, 0))
    else:
        tpb = rows_per_mod // tm
        mod_spec = pl.BlockSpec((None, 1, D), lambda i: (i // tpb, 0, 0))
    row = lambda w: pl.BlockSpec((tm, w), lambda i: (i, 0))
    out_shape = [
        jax.ShapeDtypeStruct((m, 512), BF16),
        jax.ShapeDtypeStruct((m, 512), BF16),
        jax.ShapeDtypeStruct((m, 512), BF16),
        jax.ShapeDtypeStruct((m, 512), BF16),
        jax.ShapeDtypeStruct((m, 512), BF16),
        jax.ShapeDtypeStruct((m, 512), F32),
        jax.ShapeDtypeStruct((m, LANE), F32),
        jax.ShapeDtypeStruct((m, 3 * D), BF16),
    ]
    out_specs = [row(512)] * 5 + [row(512), row(LANE), row(3 * D)]
    if emit_v:
        out_shape.append(jax.ShapeDtypeStruct((m, SGU_W), F32))
        out_specs.append(row(SGU_W))
    return pl.pallas_call(
        functools.partial(_proj_kernel, mix_rows=mix_rows, emit_v=emit_v),
        out_shape=out_shape,
        grid=(m // tm,),
        in_specs=[
            row(D), mod_spec, mod_spec,
            _const_spec(wcat.shape), _const_spec(bcat.shape), _const_spec(g1.shape), _const_spec(gsgu.shape),
            _const_spec(wmix.shape), _const_spec(bmix.shape),
        ],
        out_specs=out_specs,
        compiler_params=_cparams(("arbitrary",)),
        name="proj",
    )(x2, sc, sh, wcat, bcat, g1, gsgu, wmix, bmix)


def _mla_kernel(zc_ref, small_ref, cos_ref, sin_ref, wq_ref, wqr_ref, gq_ref, gqr_ref, wuk_ref, wuv_ref, vone_ref,
                gcq_ref, gckv_ref, gkn_ref, gkp_ref, wukt_ref,
                ckv_ref, kpe_ref, ksc_ref, *outs, sample):
    tm = zc_ref.shape[0]
    cosv = cos_ref[...]
    sinv = sin_ref[...]
    lane = lax.broadcasted_iota(jnp.int32, (tm, LANE), 1)
    scale = A_QK ** -0.5

    cq = _rms(zc_ref[:, 0:A_QR], gcq_ref[...]).astype(BF16)
    q = _dot(cq, wq_ref[...])
    qr = _dot(cq, wqr_ref[...])

    ckv = _rms(zc_ref[:, A_QR:A_QR + A_KVR], gckv_ref[...])
    ckv_ref[...] = ckv
    ckvb = ckv.astype(BF16)
    kn = _dot(ckvb, wuk_ref[...])

    small = small_ref[...]
    zkr = jnp.where((lane >= SM_KR) & (lane < SM_KR + A_ROPE), small, 0.0)
    t = zkr * gkp_ref[...]
    half = A_ROPE // 2
    rot = jnp.where(lane < SM_KR + half, pltpu.roll(t, LANE - half, 1), pltpu.roll(t, half, 1))
    kpe = t * cosv + rot * sinv
    kpe_ref[...] = kpe[:, SM_KR:SM_KR + A_ROPE]

    ksc = jnp.zeros((tm, LANE), F32)
    if sample:
        qa_ref, ql_ref = outs
    else:
        qa_ref, ka_ref, vt_ref = outs
        vt_ref[...] = (_dot_nt(wuv_ref[...], ckvb) + vone_ref[...]).astype(vt_ref.dtype)
    for h in range(AH):
        hs = slice(h * LANE, (h + 1) * LANE)
        qb = q[:, hs]
        rinv = lax.rsqrt(jnp.sum(qb * qb, axis=-1, keepdims=True) / A_QK + EPS)
        qh = (qb * rinv * gq_ref[:, hs] * cosv + qr[:, hs] * rinv * gqr_ref[:, hs] * sinv)
        kraw = kn[:, hs] + zkr
        ks = lax.rsqrt(jnp.sum(kraw * kraw, axis=-1, keepdims=True) / A_QK + EPS)
        ksc = jnp.where(lane == h, ks, ksc)
        if sample:
            qa_ref[:, hs] = qh * scale
            qn = (qh * gkn_ref[...]).astype(BF16)
            ql_ref[:, h * A_KVR:(h + 1) * A_KVR] = _dot(qn, wukt_ref[h]) * scale
        else:
            qa_ref[:, hs] = (qh * scale).astype(qa_ref.dtype)
            ka_ref[:, hs] = ((kn[:, hs] * gkn_ref[...] + kpe) * ks).astype(ka_ref.dtype)
    ksc_ref[...] = ksc[:, 0:AH]


def _mla_prep(zc, small, cos_t, sin_t, lw, *, tm, rows_per_pos, sample):
    m = zc.shape[0]
    if rows_per_pos is None:
        pos_spec = pl.BlockSpec((tm, LANE), lambda i: (i, 0))
    else:
        tpb = rows_per_pos // tm
        pos_spec = pl.BlockSpec((tm, LANE), lambda i: (i % tpb, 0))
    row = lambda w: pl.BlockSpec((tm, w), lambda i: (i, 0))
    out_shape = [
        jax.ShapeDtypeStruct((m, A_KVR), F32),
        jax.ShapeDtypeStruct((m, A_ROPE), F32),
        jax.ShapeDtypeStruct((m, AH), F32),
    ]
    out_specs = [row(A_KVR), row(A_ROPE), row(AH)]
    if sample:
        out_shape += [jax.ShapeDtypeStruct((m, AH * LANE), F32), jax.ShapeDtypeStruct((m, AH * A_KVR), F32)]
        out_specs += [row(AH * LANE), row(AH * A_KVR)]
    else:
        n_seq = m // rows_per_pos
        out_shape += [jax.ShapeDtypeStruct((m, AH * LANE), BF16)] * 2
        out_shape += [jax.ShapeDtypeStruct((n_seq * AH * LANE, rows_per_pos), BF16)]
        out_specs += [row(AH * LANE)] * 2
        out_specs += [pl.BlockSpec((AH * LANE, tm), lambda i: (i // tpb, i % tpb))]
    names =("wq", "wqr", "gq", "gqr", "wuk", "wuv", "vone", "gcq", "gckv", "gkn", "gkp", "wukt")
    ws = [lw[n] for n in names]
    return pl.pallas_call(
        functools.partial(_mla_kernel, sample=sample),
        out_shape=out_shape,
        grid=(m // tm,),
        in_specs=[row(512), row(LANE), pos_spec, pos_spec] + [_const_spec(w.shape) for w in ws],
        out_specs=out_specs,
        compiler_params=_cparams(("arbitrary",)),
        name="mla_prep",
    )(zc, small, cos_t, sin_t, *ws)


def _split3(x):
    hi = x.astype(BF16)
    r1 = x - hi.astype(F32)
    mid = r1.astype(BF16)
    lo = (r1 - mid.astype(F32)).astype(BF16)
    return hi, mid, lo


def _mlstm_kernel(q_ref, k_ref, v_ref, og_ref, small_ref, ghb_ref, c0_ref, n0_ref, m0_ref,
                  h_ref, c_ref, n_ref, m_ref, cnt_sc, *, n_chunks):
    n_seq, L = q_ref.shape[0], q_ref.shape[1]
    first = pl.program_id(1) == 0
    final = pl.program_id(1) == n_chunks - 1

    @pl.when(first)
    def _():
        m_ref[...] = m0_ref[...]
        for i in range(n_seq):
            for h in range(LH):
                cnt_sc[i, h, 0:LDV, :] = c0_ref[i, h].T
                cnt_sc[i, h, LDV:2 * LDV, :] = jnp.broadcast_to(n0_ref[i, h:h + 1, :], (LDV, LDK))

    s_idx = lax.broadcasted_iota(jnp.int32, (L, L), 0)
    t_idx = lax.broadcasted_iota(jnp.int32, (L, L), 1)
    upto = s_idx <= t_idx
    tril = jnp.where(t_idx <= s_idx, 1.0, 0.0).astype(BF16)
    lane = lax.broadcasted_iota(jnp.int32, (1, LANE), 1)
    ones = jnp.ones((L, LDV), BF16)
    for i in range(n_seq):
        _mlstm_chunk(i, q_ref, k_ref, v_ref, og_ref, small_ref, ghb_ref, h_ref, m_ref, cnt_sc,
                     upto, tril, lane, ones)

    @pl.when(final)
    def _():
        for i in range(n_seq):
            for h in range(LH):
                c_ref[i, h] = cnt_sc[i, h, 0:LDV, :].T
                n_ref[i, h:h + 1, :] = cnt_sc[i, h, LDV:LDV + 1, :]


def _mlstm_chunk(i, q_ref, k_ref, v_ref, og_ref, small_ref, ghb_ref, h_ref, m_ref, cnt_sc, upto, tril, lane, ones):
    L = q_ref.shape[1]
    small = small_ref[i]
    m_all = m_ref[i]
    m_out = m_all
    for h in range(LH):
        hs = slice(h * LDK, (h + 1) * LDK)
        zf = small[:, SM_F + h:SM_F + h + 1]
        lf = jnp.broadcast_to(jnp.minimum(zf, 0.0) - jnp.log1p(jnp.exp(-jnp.abs(zf))), (L, L))
        ig = jnp.broadcast_to(small[:, SM_I + h:SM_I + h + 1], (L, L))
        b_row = jnp.sum(jnp.where(upto, lf, 0.0), axis=0, keepdims=True)
        b_bc = functools.reduce(jnp.add, [_dot(tril, piece) for piece in _split3(lf)])
        g = ig - b_bc
        logw = jnp.where(upto, b_row + g, NEG_INF)
        m_prev = m_all[:, h:h + 1]
        m_inter = m_prev + b_row
        m_t = jnp.maximum(m_inter, jnp.max(logw, axis=0, keepdims=True))
        qh = q_ref[i, :, hs]
        kh = k_ref[i, :, hs]
        v1 = jnp.concatenate([v_ref[i, :, hs], ones], axis=1)
        smat = _dot_nt(kh, qh) * jnp.exp(logw - m_t)
        a_inter = jnp.exp(m_inter - m_t)
        cnt = cnt_sc[i, h]
        nd = (lax.dot_general(v1, smat.astype(BF16), (((0,), (0,)), ((), ())), preferred_element_type=F32)
              + a_inter * _dot_nt(cnt.astype(BF16), qh))
        hv = nd[0:LDV] / jnp.maximum(jnp.abs(nd[LDV:2 * LDV]), jnp.exp(-m_t))
        hn = hv * lax.rsqrt(jnp.mean(hv * hv, axis=0, keepdims=True) + EPS) * ghb_ref[h]
        h_ref[i, :, hs] = (og_ref[i, :, hs].astype(F32) * hn.T).astype(h_ref.dtype)
        m_new = m_t[:, L - 1:L]
        b_last = b_row[:, L - 1:L]
        decay = jnp.exp(m_prev + b_last - m_new)
        wk = (kh.astype(F32) * jnp.exp(g + (b_last - m_new))).astype(BF16)
        cnt_new = decay * cnt + lax.dot_general(v1, wk, (((0,), (0,)), ((), ())), preferred_element_type=F32)
        cnt_sc[i, h] = cnt_new
        m_out = jnp.where(lane == h, m_new, m_out)
    m_ref[i] = m_out


MLSTM_SEQS = 2


def _mlstm(q, k, v, og, small, gh, c0, n0, m0, *, n_chunks, layer):
    m = q.shape[0]
    b = c0.shape[1]
    L = m // (b * n_chunks)
    assert L == LDK == LDV, "the kernel reuses (L, L) gate matrices against (L, dk) key tiles"
    ns = MLSTM_SEQS
    seq3 = lambda a: a.reshape(b, n_chunks * L, a.shape[-1])
    row = lambda w: pl.BlockSpec((ns, L, w), lambda i, c: (i, c, 0))
    st_c = pl.BlockSpec((ns, LH, LDK, LDV), lambda i, c: (i, 0, 0, 0))
    st_n = pl.BlockSpec((ns, LH, LDK), lambda i, c: (i, 0, 0))
    st_m = pl.BlockSpec((ns, 1, LANE), lambda i, c: (i, 0, 0))
    in_c = pl.BlockSpec((None, ns, LH, LDK, LDV), lambda i, c: (layer, i, 0, 0, 0))
    in_n = pl.BlockSpec((None, ns, LH, LDK), lambda i, c: (layer, i, 0, 0))
    h3, c_f, n_f, m_f = pl.pallas_call(
        functools.partial(_mlstm_kernel, n_chunks=n_chunks),
        out_shape=[
            jax.ShapeDtypeStruct((b, n_chunks * L, 512), BF16),
            jax.ShapeDtypeStruct((b, LH, LDK, LDV), F32),
            jax.ShapeDtypeStruct((b, LH, LDK), F32),
            jax.ShapeDtypeStruct((b, 1, LANE), F32),
        ],
        grid=(b // ns, n_chunks),
        in_specs=[row(512), row(512), row(512), row(512), row(LANE), _const_spec(gh.shape), in_c, in_n, st_m],
        out_specs=[row(512), st_c, st_n, st_m],
        scratch_shapes=[pltpu.VMEM((ns, LH, 2 * LDV, LDK), F32)],
        compiler_params=_cparams(("arbitrary", "arbitrary")),
        name="mlstm",
    )(seq3(q), seq3(k), seq3(v), seq3(og), seq3(small), gh, c0, n0, m0)
    return h3.reshape(m, 512), c_f, n_f, m_f


def _attn_kernel(q_ref, k_ref, vt_ref, o_ref, p_sc, *, tq):
    t = q_ref.shape[0]
    key = lax.broadcasted_iota(jnp.int32, (tq, tq), 0)
    qry = lax.broadcasted_iota(jnp.int32, (tq, tq), 1)
    for i in range(t // tq):
        r0, r1 = i * tq, (i + 1) * tq
        st = _dot_nt(k_ref[0:r1, :], q_ref[r0:r1, :])
        st_d = jnp.where(key <= qry, st[r0:], NEG_INF)
        m = jnp.max(st_d, axis=0, keepdims=True)
        if i > 0:
            m = jnp.maximum(m, jnp.max(st[:r0], axis=0, keepdims=True))
            p_sc[0:r0] = jnp.exp(st[:r0] - m).astype(BF16)
        p_sc[r0:r1] = jnp.exp(st_d - m).astype(BF16)
        ot = _dot(vt_ref[:, 0:r1], p_sc[0:r1])
        o_ref[r0:r1, :] = (ot / ot[A_V:A_V + 1, :]).T.astype(o_ref.dtype)


def _attn_prompt(qa, ka, vt, *, b, t, tq):
    blk = pl.BlockSpec((t, LANE), lambda i, h: (i, h))
    return pl.pallas_call(
        functools.partial(_attn_kernel, tq=tq),
        out_shape=jax.ShapeDtypeStruct(qa.shape, BF16),
        grid=(b, AH),
        in_specs=[blk, blk, pl.BlockSpec((LANE, t), lambda i, h: (i * AH + h, 0))],
        out_specs=blk,
        scratch_shapes=[pltpu.VMEM((t, tq), BF16)],
        compiler_params=_cparams(("arbitrary", "arbitrary")),
        name="attn_prompt",
    )(qa, ka, vt)


def _paged_kernel(pt_ref, ql_ref, qp_ref, cn_ref, kpn_ref, ksn_ref, ckv_hbm, kpe_hbm, ks_hbm,
                  o_ref, cbuf, kpbuf, ksbuf, sems, m_sc, l_sc, acc_sc, *, layer, n_per_step, n_chains, t_new):
    j = pl.program_id(1)
    steps = pl.num_programs(1)
    step = pl.program_id(0) * steps + j
    last = pl.num_programs(0) * steps - 1
    slot = step % PAGED_SLOTS
    ql = ql_ref[...]
    qp = qp_ref[...]
    nr = ql.shape[0]

    def page_copies(step_, n):
        slot_ = step_ % PAGED_SLOTS
        pg = pt_ref[jnp.minimum(step_, last) * n_per_step + n]
        return (pltpu.make_async_copy(ckv_hbm.at[layer, pg], cbuf.at[slot_, n], sems.at[slot_, 0]),
                pltpu.make_async_copy(kpe_hbm.at[layer, pg], kpbuf.at[slot_, n], sems.at[slot_, 1]),
                pltpu.make_async_copy(ks_hbm.at[layer, pg], ksbuf.at[slot_, n], sems.at[slot_, 2]))

    @pl.when(step == 0)
    def _():
        for ahead in range(PAGED_SLOTS - 1):
            for n in range(n_per_step):
                for cp in page_copies(ahead, n):
                    cp.start()

    for n in range(n_per_step):
        for cp in page_copies(step, n):
            cp.wait()

    @pl.when(j == 0)
    def _():
        cn = cn_ref[...].astype(BF16)
        s = (_dot_nt(ql, cn) + _dot(qp, kpn_ref[...].astype(BF16))) * ksn_ref[...]
        nk = cn.shape[0]
        r_t = lax.broadcasted_iota(jnp.int32, (nr, nk), 0) // AH
        key = lax.broadcasted_iota(jnp.int32, (nr, nk), 1)
        s = jnp.where((key <= r_t) & (key < t_new), s, NEG_INF)
        m0 = jnp.max(s, axis=-1, keepdims=True)
        p = jnp.exp(s - m0)
        m_sc[0] = m0
        l_sc[0] = jnp.sum(p, axis=-1, keepdims=True)
        acc_sc[0] = _dot(p.astype(BF16), cn)
        for c in range(1, n_chains):
            m_sc[c] = m0
            l_sc[c] = jnp.zeros_like(m0)
            acc_sc[c] = jnp.zeros((nr, A_KVR), F32)

    per_chain = n_per_step // n_chains
    cbs, ss = [], []
    for n in range(n_per_step):
        for cp in page_copies(step + PAGED_SLOTS - 1, n):
            cp.start()
        cb = cbuf[slot, n].astype(BF16)
        kpt = kpbuf[slot, n].astype(BF16)
        ks = ksbuf[slot, n]
        kst = jnp.concatenate([ks] * t_new, axis=0)
        cbs.append(cb)
        ss.append((_dot_nt(ql, cb) + _dot(qp, kpt)) * kst)
    finals = []
    for c in range(n_chains):
        sc = ss[c * per_chain:(c + 1) * per_chain]
        cc = cbs[c * per_chain:(c + 1) * per_chain]
        m_i = m_sc[c]
        m_new = jnp.maximum(m_i, jnp.max(functools.reduce(jnp.maximum, sc), axis=-1, keepdims=True))
        alpha = jnp.exp(m_i - m_new)
        ps = [jnp.exp(s - m_new) for s in sc]
        pv = functools.reduce(jnp.add, [_dot(p.astype(BF16), cb) for p, cb in zip(ps, cc)])
        l_i = alpha * l_sc[c] + jnp.sum(functools.reduce(jnp.add, ps), axis=-1, keepdims=True)
        acc = alpha * acc_sc[c] + pv
        m_sc[c] = m_new
        l_sc[c] = l_i
        acc_sc[c] = acc
        finals.append((m_new, l_i, acc))

    @pl.when(j == steps - 1)
    def _():
        m_f = functools.reduce(jnp.maximum, [f[0] for f in finals])
        ws = [jnp.exp(f[0] - m_f) for f in finals]
        l_f = functools.reduce(jnp.add, [w * f[1] for w, f in zip(ws, finals)])
        acc_f = functools.reduce(jnp.add, [w * f[2] for w, f in zip(ws, finals)])
        o_ref[...] = acc_f / l_f

    @pl.when(step == last)
    def _():
        for ahead in range(1, PAGED_SLOTS):
            for n in range(n_per_step):
                for cp in page_copies(last + ahead, n):
                    cp.wait()


def _attn_paged(page_table, qlat, qp, c_new, kp_new_t, ks_new_t, cache_ckv, cache_kpe_t, cache_ks_t, *, layer, t_new,
                n_per_step, n_chains):
    bs, nr, _ = qlat.shape
    n_pages = page_table.shape[1]
    nk = c_new.shape[1]
    steps = n_pages // n_per_step
    pt_flat = page_table.reshape(-1)

    per_b = lambda s1, s2: pl.BlockSpec((None, s1, s2), lambda b, j, pt: (b, 0, 0))
    hbm = pl.BlockSpec(memory_space=pl.ANY)
    grid_spec = pltpu.PrefetchScalarGridSpec(
        num_scalar_prefetch=1,
        grid=(bs, steps),
        in_specs=[per_b(nr, A_KVR), per_b(nr, A_ROPE), per_b(nk, A_KVR), per_b(A_ROPE, nk), per_b(nr, nk),
                  hbm, hbm, hbm],
        out_specs=per_b(nr, A_KVR),
        scratch_shapes=[
            pltpu.VMEM((PAGED_SLOTS, n_per_step, PAGE, A_KVR), F32),
            pltpu.VMEM((PAGED_SLOTS, n_per_step, A_ROPE, PAGE), F32),
            pltpu.VMEM((PAGED_SLOTS, n_per_step, AH, PAGE), F32),
            pltpu.SemaphoreType.DMA((PAGED_SLOTS, 3)),
            pltpu.VMEM((n_chains, nr, 1), F32), pltpu.VMEM((n_chains, nr, 1), F32),
            pltpu.VMEM((n_chains, nr, A_KVR), F32),
        ],
    )
    return pl.pallas_call(
        functools.partial(_paged_kernel, layer=layer, n_per_step=n_per_step, n_chains=n_chains, t_new=t_new),
        out_shape=jax.ShapeDtypeStruct((bs, nr, A_KVR), F32),
        grid_spec=grid_spec,
        compiler_params=_cparams(("arbitrary", "arbitrary")),
        name="attn_paged",
    )(pt_flat, qlat, qp, c_new, kp_new_t, ks_new_t, cache_ckv, cache_kpe_t, cache_ks_t)


def _uv_kernel(x_ref, w_ref, o_ref):
    o_ref[...] = _dot(x_ref[...].astype(BF16), w_ref[...]).astype(o_ref.dtype)


def _uv_proj(olat_h, wuvp):
    _, m, _ = olat_h.shape
    return pl.pallas_call(
        _uv_kernel,
        out_shape=jax.ShapeDtypeStruct((m, AH * LANE), BF16),
        grid=(AH,),
        in_specs=[pl.BlockSpec((None, m, A_KVR), lambda h: (h, 0, 0)),
                  pl.BlockSpec((None, A_KVR, LANE), lambda h: (h, 0, 0))],
        out_specs=pl.BlockSpec((m, LANE), lambda h: (0, h)),
        compiler_params=_cparams(("arbitrary",)),
        name="uv_proj",
    )(olat_h, wuvp)


def _merge_kernel(x_ref, g1_ref, oa_ref, ob_ref, oc_ref, gates_ref, wa_ref, wb_ref, wc_ref, wo_ref, o_ref):
    mix = gates_ref[:, 0:D].astype(F32) * _dot(oa_ref[...], wa_ref[...])
    mix += gates_ref[:, D:2 * D].astype(F32) * _dot(ob_ref[...], wb_ref[...])
    mix += gates_ref[:, 2 * D:3 * D].astype(F32) * _dot(oc_ref[...], wc_ref[...])
    o_ref[...] = x_ref[...] + g1_ref[...] * _dot(mix.astype(BF16), wo_ref[...])


def _mod_spec(tm, rows_per_mod):
    if rows_per_mod is None:
        return pl.BlockSpec((tm, D), lambda i: (i, 0))
    tpb = rows_per_mod // tm
    return pl.BlockSpec((None, 1, D), lambda i: (i // tpb, 0, 0))


def _merge(x2, g1, oa, ob, oc, gates, wa, wb, wc, wo, *, tm, rows_per_mod):
    m = x2.shape[0]
    row = lambda w: pl.BlockSpec((tm, w), lambda i: (i, 0))
    return pl.pallas_call(
        _merge_kernel,
        out_shape=jax.ShapeDtypeStruct((m, D), F32),
        grid=(m // tm,),
        in_specs=[row(D), _mod_spec(tm, rows_per_mod), row(512), row(512), row(AH * LANE), row(3 * D),
                  _const_spec(wa.shape), _const_spec(wb.shape), _const_spec(wc.shape), _const_spec(wo.shape)],
        out_specs=row(D),
        compiler_params=_cparams(("arbitrary",)),
        name="merge",
    )(x2, g1, oa, ob, oc, gates, wa, wb, wc, wo)


def _moe_kernel(x_ref, sc_ref, sh_ref, g2_ref, gn_ref, wr1_ref, wr2_ref, br_ref, wg_ref, wu_ref, wd_ref, o_ref):
    tm = x_ref.shape[0]
    x = x_ref[...]
    h2 = _rms(x, gn_ref[...]) * (1.0 + sc_ref[...]) + sh_ref[...]
    hi = h2.astype(BF16)
    lo = (h2 - hi.astype(F32)).astype(BF16)
    r = _dot(hi, wr1_ref[...]) + _dot(lo, wr2_ref[...])
    rt = r.T
    logits = rt[0:32] + rt[32:64]
    scores = _sigmoid(logits)
    sel = scores + br_ref[...]
    e = [sel[8 * p:8 * p + 8] for p in range(GRP_E)]
    sc = [scores[8 * p:8 * p + 8] for p in range(GRP_E)]
    a, b = jnp.maximum(e[0], e[1]), jnp.minimum(e[0], e[1])
    c, d = jnp.maximum(e[2], e[3]), jnp.minimum(e[2], e[3])
    top1 = jnp.maximum(a, c)
    top2 = jnp.maximum(jnp.minimum(a, c), jnp.maximum(b, d))
    rowi = lax.broadcasted_iota(jnp.int32, (8, tm), 0).astype(F32)
    gs = jnp.where(rowi < N_GRP, top1 + top2, NEG_INF)
    gmax = jnp.max(gs, axis=0, keepdims=True)
    gidx = jnp.min(jnp.where(gs == gmax, rowi, 8.0), axis=0, keepdims=True)
    oh = rowi == gidx
    s = [jnp.sum(jnp.where(oh, e[p], 0.0), axis=0, keepdims=True) for p in range(GRP_E)]
    w = [jnp.sum(jnp.where(oh, sc[p], 0.0), axis=0, keepdims=True) for p in range(GRP_E)]

    def first_argmax(vals):
        best = jnp.maximum(jnp.maximum(vals[0], vals[1]), jnp.maximum(vals[2], vals[3]))
        return jnp.where(vals[0] == best, 0, jnp.where(vals[1] == best, 1, jnp.where(vals[2] == best, 2, 3)))

    i1 = first_argmax(s)
    i2 = first_argmax([jnp.where(i1 == p, NEG_INF, s[p]) for p in range(GRP_E)])
    w1 = sum(jnp.where(i1 == p, w[p], 0.0) for p in range(GRP_E))
    w2 = sum(jnp.where(i2 == p, w[p], 0.0) for p in range(GRP_E))
    den = w1 + w2
    w1 = w1 / den
    w2 = w2 / den
    comb = [jnp.where(oh, jnp.where(i1 == p, w1, 0.0) + jnp.where(i2 == p, w2, 0.0), 0.0) for p in range(GRP_E)]
    comb_t = jnp.concatenate(comb + [jnp.zeros((LANE - 8 * GRP_E, tm), F32)], axis=0)
    combine = comb_t.T

    acc = jnp.zeros((tm, D), F32)
    for g in range(N_GRP):
        acts = []
        for p in range(GRP_E):
            ex = g * GRP_E + p
            hg = _dot(hi, wg_ref[ex])
            hu = _dot(hi, wu_ref[ex])
            cw = combine[:, 8 * p + g:8 * p + g + 1]
            acts.append((hg * _sigmoid(hg) * hu * cw).astype(BF16))
        acc += _dot(jnp.concatenate(acts, axis=1), wd_ref[g])
    o_ref[...] = x + g2_ref[...] * acc


def _moe(x2, sc, sh, g2, gn, wr1, wr2, br, wg, wu, wd, *, tm, rows_per_mod):
    m = x2.shape[0]
    row = lambda w: pl.BlockSpec((tm, w), lambda i: (i, 0))
    ms = _mod_spec(tm, rows_per_mod)
    return pl.pallas_call(
        _moe_kernel,
        out_shape=jax.ShapeDtypeStruct((m, D), F32),
        grid=(m // tm,),
        in_specs=[row(D), ms, ms, ms, _const_spec(gn.shape), _const_spec(wr1.shape), _const_spec(wr2.shape),
                  _const_spec(br.shape), _const_spec(wg.shape), _const_spec(wu.shape), _const_spec(wd.shape)],
        out_specs=row(D),
        compiler_params=_cparams(("arbitrary",)),
        name="moe",
    )(x2, sc, sh, g2, gn, wr1, wr2, br, wg, wu, wd)


def _head_block(nope, r1, r2):
    pad = jnp.zeros(nope.shape[:-1] + (LANE - A_QK,), nope.dtype)
    blk = jnp.concatenate([nope, r1, r2, pad], axis=-1)
    return blk.reshape(blk.shape[:-2] + (AH * LANE,))


def _prep_layer(l, w_in, b_in, w_gate, b_gate, g_norm1, g_norm2, g_sgu, g_mlstm, g_cq, w_q_up, g_ckv, w_uk, w_uv,
                g_qn, g_kn, w_br, w_o, w_exp_gate, w_exp_up, w_exp_down):
    half = A_ROPE // 2
    wi, bi = w_in[l], b_in[l]

    def cat_cols(a, gate):
        z = lambda n: jnp.zeros(a.shape[:-1] + (n,), a.dtype)
        small = jnp.concatenate([z(SM_KR), a[..., 3592:3624], a[..., 3072:3080], z(LANE - SM_F - LH)], axis=-1)
        return jnp.concatenate([a[..., :3072], a[..., 3080:3592], small, gate], axis=-1)

    lw = {}
    lw["wcat"] = cat_cols(wi, w_gate[l]).astype(BF16)
    lw["bcat"] = cat_cols(bi, b_gate[l]).reshape(1, N_CAT)
    lw["g1"] = g_norm1[l].reshape(1, D)
    lw["g2"] = g_norm2[l].reshape(1, D)
    lw["gsgu"] = g_sgu[l].reshape(1, SGU_W)
    lw["gh"] = jnp.broadcast_to(g_mlstm[l][:, :, None], (LH, LDV, LCHUNK))
    wq = w_q_up[l]
    nope, r1, r2 = wq[..., :A_NOPE], wq[..., A_NOPE:A_NOPE + half], wq[..., A_NOPE + half:]
    lw["wq"] = _head_block(nope, r1, r2).astype(BF16)
    lw["wqr"] = _head_block(jnp.zeros_like(nope), r2, r1).astype(BF16)
    gq = jnp.broadcast_to(g_qn[l], (AH, A_QK))
    gn_, g1_, g2_ = gq[:, :A_NOPE], gq[:, A_NOPE:A_NOPE + half], gq[:, A_NOPE + half:]
    lw["gq"] = _head_block(gn_, g1_, g2_).reshape(1, AH * LANE)
    lw["gqr"] = _head_block(jnp.zeros_like(gn_), g2_, g1_).reshape(1, AH * LANE)
    zr = jnp.zeros((A_KVR, AH, half), F32)
    lw["wuk"] = _head_block(w_uk[l], zr, zr).astype(BF16)
    uv = jnp.concatenate([w_uv[l], jnp.zeros((A_KVR, AH, LANE - A_V), F32)], axis=-1)
    lw["wuv"] = uv.reshape(A_KVR, AH * LANE).T.astype(BF16)
    lw["wuvp"] = jnp.transpose(uv, (1, 0, 2)).astype(BF16)
    lw["vone"] = jnp.tile((jnp.arange(LANE) == A_V).astype(F32), AH).reshape(AH * LANE, 1)
    lw["gcq"] = g_cq[l].reshape(1, A_QR)
    lw["gckv"] = g_ckv[l].reshape(1, A_KVR)
    gk = g_kn[l]
    lw["gkn"] = jnp.concatenate([gk[:A_NOPE], jnp.zeros((LANE - A_NOPE,), F32)]).reshape(1, LANE)
    lw["gkp"] = jnp.concatenate([jnp.zeros((SM_KR,), F32), gk[A_NOPE:], jnp.zeros((LANE - A_QK,), F32)]).reshape(1, LANE)
    ukt = jnp.transpose(w_uk[l], (1, 2, 0))
    lw["wukt"] = jnp.concatenate([ukt, jnp.zeros((AH, LANE - A_NOPE, A_KVR), F32)], axis=1).astype(BF16)
    lw["wa"] = w_br[l, 0].astype(BF16)
    lw["wb"] = w_br[l, 1].astype(BF16)
    wc = w_br[l, 2].reshape(AH, A_V, D)
    lw["wc"] = jnp.concatenate([wc, jnp.zeros((AH, LANE - A_V, D), F32)], axis=1).reshape(AH * LANE, D).astype(BF16)
    lw["wo"] = w_o[l].astype(BF16)
    lw["wg"] = w_exp_gate[l].astype(BF16)
    lw["wu"] = w_exp_up[l].astype(BF16)
    lw["wd"] = w_exp_down[l].reshape(N_GRP, GRP_E * D_EXP, D).astype(BF16)
    return lw


def _prep_router(w_router, b_router):
    wr = w_router.reshape(D, N_GRP, GRP_E).transpose(0, 2, 1)
    wr = jnp.concatenate([wr, jnp.zeros((D, GRP_E, 8 - N_GRP), F32)], axis=-1).reshape(D, 8 * GRP_E)
    hi = wr.astype(BF16)
    lo = (wr - hi.astype(F32)).astype(BF16)
    z = lambda n: jnp.zeros((D, n), BF16)
    wr1 = jnp.concatenate([hi, lo, z(LANE - 64)], axis=1)
    wr2 = jnp.concatenate([hi, z(LANE - 32)], axis=1)
    br = b_router.reshape(N_GRP, GRP_E).T
    br = jnp.concatenate([br, jnp.zeros((GRP_E, 8 - N_GRP), F32)], axis=-1).reshape(8 * GRP_E, 1)
    return wr1, wr2, br


def _rope_tables(pos):
    half = A_ROPE // 2
    freqs = ROPE_BASE ** (-jnp.arange(half, dtype=F32) / half)
    ang = pos[:, None] * freqs[None, :]
    cos, sin = jnp.cos(ang), jnp.sin(ang)
    n = pos.shape[0]
    cos_t = jnp.concatenate([jnp.ones((n, A_NOPE), F32), cos, cos, jnp.zeros((n, LANE - A_QK), F32)], axis=1)
    sin_t = jnp.concatenate([jnp.zeros((n, A_NOPE), F32), -sin, sin, jnp.zeros((n, LANE - A_QK), F32)], axis=1)
    return cos_t, sin_t


def _mix_weights(w_spatial_l, b_spatial_l, chunk, reps):
    w = jnp.where(jnp.tril(jnp.ones((chunk, chunk), dtype=bool))[None], w_spatial_l[:, :chunk, :chunk], 0.0)
    if reps > 1:
        idx = jnp.arange(reps * chunk)
        pos = jax.nn.one_hot(idx % chunk, chunk, dtype=F32)
        same = (idx[:, None] // chunk) == (idx[None, :] // chunk)
        w = jnp.where(same[None], jnp.einsum("rt,gts,cs->grc", pos, w, pos, precision=lax.Precision.HIGHEST), 0.0)
    bias = jnp.tile(b_spatial_l[:, :chunk].T, (reps, 1))
    bias = jnp.repeat(bias, SGU_C, axis=1)
    return w.astype(BF16), bias


def _layer_prompt(x2, mod, lw, router, tables, wmix, bmix, b, t):
    m = b * t
    tm = 512
    sh1, sc1, g1, sh2, sc2, g2 = [a.reshape(b, 1, D) for a in jnp.split(mod, 6, axis=-1)]
    oa, q, k, v, og, zc, small, gates = _proj(x2, sc1, sh1, lw["wcat"], lw["bcat"], lw["g1"], lw["gsgu"], wmix, bmix,
                                              tm=tm, rows_per_mod=t, emit_v=False)
    ckv, kpe, ksc, qa, ka, va = _mla_prep(zc, small, tables[0], tables[1], lw, tm=tm, rows_per_pos=t, sample=False)
    n_chunks = t // LCHUNK
    c0 = jnp.zeros((b, LH, LDK, LDV), F32)
    n0 = jnp.zeros((b, LH, LDK), F32)
    m0 = jnp.zeros((b, 1, LANE), F32)
    ob, c_f, n_f, m_f = _mlstm(q, k, v, og, small, lw["gh"], c0[None], n0[None], m0, n_chunks=n_chunks, layer=0)
    oc = _attn_prompt(qa, ka, va, b=b, t=t, tq=min(t, 256))
    x1 = _merge(x2, g1, oa, ob, oc, gates, lw["wa"], lw["wb"], lw["wc"], lw["wo"], tm=tm, rows_per_mod=t)
    x_out = _moe(x1, sc2, sh2, g2, lw["g2"], *router, lw["wg"], lw["wu"], lw["wd"], tm=tm, rows_per_mod=t)
    state = (ckv.reshape(b, t, A_KVR), kpe.reshape(b, t, A_ROPE), ksc.reshape(b, t, AH), c_f, n_f, m_f[:, 0, :LH])
    return x_out, state


def _layer_sample(x2, mod, lw, router, tables, wmix, bmix, b, t, st_c, st_n, st_m, caches, page_table, layer):
    m = b * t
    tm = m
    mods = [jnp.repeat(a, t, axis=0) for a in jnp.split(mod, 6, axis=-1)]
    sh1, sc1, g1, sh2, sc2, g2 = mods
    oa, q, k, v, og, zc, small, gates, v_sgu = _proj(x2, sc1, sh1, lw["wcat"], lw["bcat"], lw["g1"], lw["gsgu"], wmix,
                                                     bmix, tm=tm, rows_per_mod=None, emit_v=True)
    ckv, kpe, ksc, qa, qlat = _mla_prep(zc, small, tables[0], tables[1], lw, tm=tm, rows_per_pos=None, sample=True)

    lpad = LCHUNK

    def pad_rows(a, fill=None):
        a3 = a.reshape(b, t, a.shape[-1])
        if fill is None:
            padv = jnp.zeros((b, lpad - t, a.shape[-1]), a.dtype)
        else:
            padv = jnp.broadcast_to(fill, (b, lpad - t, a.shape[-1])).astype(a.dtype)
        return jnp.concatenate([a3, padv], axis=1).reshape(b * lpad, a.shape[-1])

    lane = jnp.arange(LANE)
    inert = jnp.where((lane >= SM_I) & (lane < SM_I + LH), -1e30, jnp.where((lane >= SM_F) & (lane < SM_F + LH), 1e30, 0.0))
    m0 = jnp.concatenate([st_m, jnp.zeros((b, LANE - LH), F32)], axis=1).reshape(b, 1, LANE)
    ob_p, c_f, n_f, m_f = _mlstm(pad_rows(q), pad_rows(k), pad_rows(v), pad_rows(og), pad_rows(small, inert.astype(F32)),
                                 lw["gh"], st_c, st_n, m0, n_chunks=1, layer=layer)
    ob = ob_p.reshape(b, lpad, 512)[:, :t].reshape(m, 512)

    nr = t * AH
    nk = 8
    ql3 = qlat.reshape(b, nr, A_KVR).astype(BF16)
    qp3 = qa.reshape(b, t, AH, LANE)[..., A_NOPE:A_QK].reshape(b, nr, A_ROPE).astype(BF16)
    padk = lambda a: jnp.concatenate([a.reshape(b, t, -1), jnp.zeros((b, nk - t, a.shape[-1]), F32)], axis=1)
    ks_t = jnp.transpose(padk(ksc), (0, 2, 1))
    ks_t = jnp.tile(ks_t, (1, t, 1))
    kp_t = jnp.transpose(padk(kpe), (0, 2, 1))
    olat = _attn_paged(page_table, ql3, qp3, padk(ckv), kp_t, ks_t, *caches, layer=layer, t_new=t, n_per_step=32,
                       n_chains=4)
    olat_h = jnp.transpose(olat.reshape(b, t, AH, A_KVR), (2, 0, 1, 3)).reshape(AH, m, A_KVR)
    oc = _uv_proj(olat_h, lw["wuvp"])

    x1 = _merge(x2, g1, oa, ob, oc, gates, lw["wa"], lw["wb"], lw["wc"], lw["wo"], tm=tm, rows_per_mod=None)
    x_out = _moe(x1, sc2, sh2, g2, lw["g2"], *router, lw["wg"], lw["wu"], lw["wd"], tm=tm, rows_per_mod=None)
    state = (ckv.reshape(b, t, A_KVR), kpe.reshape(b, t, A_ROPE), ksc.reshape(b, t, AH), c_f, n_f, m_f[:, 0, :LH],
             v_sgu.reshape(b, t, SGU_W))
    return x_out, state


def kernel(x_prompt, x_sample, c_prompt, c_sample, cache_ckv, cache_kpe, cache_kscale, page_table, state_C, state_n, state_m, w_ada, b_ada, g_norm1, g_norm2, w_in, b_in, g_sgu, w_spatial, b_spatial, g_mlstm, g_cq, w_q_up, g_ckv, w_uk, w_uv, g_qn, g_kn, w_br, w_gate, b_gate, w_o, w_router, b_router, w_exp_gate, w_exp_up, w_exp_down):
    bp, tp, _ = x_prompt.shape
    bs, ts, _ = x_sample.shape
    depth = w_in.shape[0]
    past_len = page_table.shape[1] * PAGE

    mod_all = _ada(jnp.concatenate([c_prompt, c_sample], axis=0), w_ada, b_ada)
    router = _prep_router(w_router, b_router)
    tab_p = _rope_tables(jnp.arange(tp, dtype=F32))
    pos_s = jnp.tile(jnp.arange(ts, dtype=F32) + past_len, bs)
    tab_s = _rope_tables(pos_s)
    caches = (cache_ckv, jnp.swapaxes(cache_kpe, 2, 3), jnp.swapaxes(cache_kscale, 2, 3))

    xp = x_prompt.reshape(bp * tp, D)
    xs = x_sample.reshape(bs * ts, D)
    st_p, st_s = [], []
    for l in range(depth):
        lw = _prep_layer(l, w_in, b_in, w_gate, b_gate, g_norm1, g_norm2, g_sgu, g_mlstm, g_cq, w_q_up, g_ckv, w_uk,
                         w_uv, g_qn, g_kn, w_br, w_o, w_exp_gate, w_exp_up, w_exp_down)
        chunk_p = min(tp, SGU_C)
        wmix_p, bmix_p = _mix_weights(w_spatial[l], b_spatial[l], chunk_p, 1)
        wmix_s, bmix_s = _mix_weights(w_spatial[l], b_spatial[l], ts, bs)
        xp, sp = _layer_prompt(xp, mod_all[l, :bp], lw, router, tab_p, wmix_p, bmix_p, bp, tp)
        xs, ss = _layer_sample(xs, mod_all[l, bp:], lw, router, tab_s, wmix_s, bmix_s, bs, ts, state_C, state_n,
                               state_m[l], caches, page_table, l)
        st_p.append(sp)
        st_s.append(ss)
    stack = lambda sts, i: jnp.stack([s[i] for s in sts])
    return (xp.reshape(bp, tp, D), xs.reshape(bs, ts, D),
            stack(st_p, 0), stack(st_p, 1), stack(st_p, 2), stack(st_p, 3), stack(st_p, 4), stack(st_p, 5),
            stack(st_s, 0), stack(st_s, 1), stack(st_s, 2), stack(st_s, 3), stack(st_s, 4), stack(st_s, 5),
            stack(st_s, 6))
```

```python
import functools
import math

import jax
import jax.numpy as jnp
from jax import lax
from jax.experimental import pallas as pl
from jax.experimental.pallas import tpu as pltpu

F32 = jnp.float32
BF16 = jnp.bfloat16
NEG_INF = float("-inf")

EPS = 1e-6
D = 1024
SGU_G = 4
SGU_C = 128
SGU_W = SGU_G * SGU_C
LH = 4
LDK = 128
LDV = 128
LCHUNK = 128
AH = 8
A_QR = 256
A_KVR = 256
A_NOPE = 64
A_ROPE = 32
A_QK = A_NOPE + A_ROPE
A_V = 64
ROPE_BASE = 10000.0
PAGE = 128
N_EXP = 16
N_GRP = 4
GRP_E = 4
D_EXP = 256

LANE = 128
BF16_ROWS = 16
PAGED_SLOTS = 3
C_ZU, C_ZV, C_ZQ, C_ZK, C_ZVV, C_ZO = 0, 512, 1024, 1536, 2048, 2560
C_ZC = 3072
C_SMALL = 3584
C_GATE = 3712
N_CAT = C_GATE + 3 * D
SM_KR, SM_I, SM_F = 64, 96, 100

VMEM_LIMIT = 56 * 1024 * 1024


def _cparams(sem):
    return pltpu.CompilerParams(dimension_semantics=sem, vmem_limit_bytes=VMEM_LIMIT)


def _const_spec(shape):
    nd = len(shape)
    return pl.BlockSpec(shape, lambda *_: (0,) * nd, pipeline_mode=pl.Buffered(1))


def _sigmoid(x):
    return 0.5 * jnp.tanh(0.5 * x) + 0.5


def _dot(a, b):
    return jnp.dot(a, b, preferred_element_type=F32)


def _dot_nt(a, b):
    return lax.dot_general(a, b, (((1,), (1,)), ((), ())), preferred_element_type=F32)


def _rms(x, g):
    return x * lax.rsqrt(jnp.mean(x * x, axis=-1, keepdims=True) + EPS) * g


def _ada_kernel(c_ref, w_ref, b_ref, o_ref):
    c = c_ref[...]
    s = (c * _sigmoid(c)).astype(BF16)
    o_ref[...] = _dot(s, w_ref[...].astype(BF16)) + b_ref[...]


def _ada(c_all, w_ada, b_ada):
    depth, _, n = w_ada.shape
    bc = c_all.shape[0]
    tn = 1536
    return pl.pallas_call(
        _ada_kernel,
        out_shape=jax.ShapeDtypeStruct((depth, bc, n), F32),
        grid=(depth, n // tn),
        in_specs=[
            pl.BlockSpec((bc, D), lambda l, j: (0, 0)),
            pl.BlockSpec((None, D, tn), lambda l, j: (l, 0, j)),
            pl.BlockSpec((None, 1, tn), lambda l, j: (l, 0, j)),
        ],
        out_specs=pl.BlockSpec((None, bc, tn), lambda l, j: (l, 0, j)),
        compiler_params=_cparams(("arbitrary", "arbitrary")),
        name="ada",
    )(c_all, w_ada, b_ada.reshape(depth, 1, n))


def _proj_kernel(x_ref, sc_ref, sh_ref, w_ref, b_ref, g1_ref, gsgu_ref, wmix_ref, bmix_ref,
                 oa_ref, q_ref, k_ref, v_ref, og_ref, zc_ref, small_ref, gates_ref, *rest, mix_rows, emit_v):
    tm = x_ref.shape[0]
    x = x_ref[...]
    h = _rms(x, g1_ref[...]) * (1.0 + sc_ref[...]) + sh_ref[...]
    hb = h.astype(BF16)

    def sec(a, b):
        return _dot(hb, w_ref[:, a:b]) + b_ref[:, a:b]

    u = jax.nn.gelu(sec(C_ZU, C_ZU + SGU_W))
    vv = _rms(jax.nn.gelu(sec(C_ZV, C_ZV + SGU_W)), gsgu_ref[...])
    if emit_v:
        rest[0][...] = vv
    vb = vv.astype(BF16)
    for g in range(SGU_G):
        cs = slice(g * SGU_C, (g + 1) * SGU_C)
        for c in range(tm // mix_rows):
            rs = slice(c * mix_rows, (c + 1) * mix_rows)
            z = _dot(wmix_ref[g], vb[rs, cs]) + bmix_ref[:, cs]
            oa_ref[rs, cs] = (u[rs, cs] * z).astype(oa_ref.dtype)

    q_ref[...] = sec(C_ZQ, C_ZQ + 512).astype(q_ref.dtype)
    k_ref[...] = (sec(C_ZK, C_ZK + 512) * (LDK ** -0.5)).astype(k_ref.dtype)
    v_ref[...] = sec(C_ZVV, C_ZVV + 512).astype(v_ref.dtype)
    og_ref[...] = _sigmoid(sec(C_ZO, C_ZO + 512)).astype(og_ref.dtype)
    zc_ref[...] = sec(C_ZC, C_ZC + 512)
    small_ref[...] = sec(C_SMALL, C_SMALL + LANE)
    for n in range(3):
        a = C_GATE + n * D
        gates_ref[:, n * D:(n + 1) * D] = _sigmoid(sec(a, a + D)).astype(gates_ref.dtype)


def _proj(x2, sc, sh, wcat, bcat, g1, gsgu, wmix, bmix, *, tm, rows_per_mod, emit_v):
    m = x2.shape[0]
    mix_rows = wmix.shape[-1]
    if rows_per_mod is None:
        mod_spec = pl.BlockSpec((tm, D), lambda i: (i, 0))
    else:
        tpb = rows_per_mod // tm
        mod_spec = pl.BlockSpec((None, 1, D), lambda i: (i // tpb, 0, 0))
    row = lambda w: pl.BlockSpec((tm, w), lambda i: (i, 0))
    out_shape = [
        jax.ShapeDtypeStruct((m, 512), BF16),
        jax.ShapeDtypeStruct((m, 512), BF16),
        jax.ShapeDtypeStruct((m, 512), BF16),
        jax.ShapeDtypeStruct((m, 512), BF16),
        jax.ShapeDtypeStruct((m, 512), BF16),
        jax.ShapeDtypeStruct((m, 512), F32),
        jax.ShapeDtypeStruct((m, LANE), F32),
        jax.ShapeDtypeStruct((m, 3 * D), BF16),
    ]
    out_specs = [row(512)] * 5 + [row(512), row(LANE), row(3 * D)]
    if emit_v:
        out_shape.append(jax.ShapeDtypeStruct((m, SGU_W), F32))
        out_specs.append(row(SGU_W))
    return pl.pallas_call(
        functools.partial(_proj_kernel, mix_rows=mix_rows, emit_v=emit_v),
        out_shape=out_shape,
        grid=(m // tm,),
        in_specs=[
            row(D), mod_spec, mod_spec,
            _const_spec(wcat.shape), _const_spec(bcat.shape), _const_spec(g1.shape), _const_spec(gsgu.shape),
            _const_spec(wmix.shape), _const_spec(bmix.shape),
        ],
        out_specs=out_specs,
        compiler_params=_cparams(("arbitrary",)),
        name="proj",
    )(x2, sc, sh, wcat, bcat, g1, gsgu, wmix, bmix)


def _mla_kernel(zc_ref, small_ref, cos_ref, sin_ref, wq_ref, wqr_ref, gq_ref, gqr_ref, wuk_ref, wuv_ref, vone_ref,
                gcq_ref, gckv_ref, gkn_ref, gkp_ref, wukt_ref,
                ckv_ref, kpe_ref, ksc_ref, *outs, sample):
    tm = zc_ref.shape[0]
    cosv = cos_ref[...]
    sinv = sin_ref[...]
    lane = lax.broadcasted_iota(jnp.int32, (tm, LANE), 1)
    scale = A_QK ** -0.5

    cq = _rms(zc_ref[:, 0:A_QR], gcq_ref[...]).astype(BF16)
    q = _dot(cq, wq_ref[...])
    qr = _dot(cq, wqr_ref[...])

    ckv = _rms(zc_ref[:, A_QR:A_QR + A_KVR], gckv_ref[...])
    ckv_ref[...] = ckv
    ckvb = ckv.astype(BF16)
    kn = _dot(ckvb, wuk_ref[...])

    small = small_ref[...]
    zkr = jnp.where((lane >= SM_KR) & (lane < SM_KR + A_ROPE), small, 0.0)
    t = zkr * gkp_ref[...]
    half = A_ROPE // 2
    rot = jnp.where(lane < SM_KR + half, pltpu.roll(t, LANE - half, 1), pltpu.roll(t, half, 1))
    kpe = t * cosv + rot * sinv
    kpe_ref[...] = kpe[:, SM_KR:SM_KR + A_ROPE]

    ksc = jnp.zeros((tm, LANE), F32)
    if sample:
        qa_ref, ql_ref = outs
    else:
        qa_ref, ka_ref, vt_ref = outs
        vt_ref[...] = (_dot_nt(wuv_ref[...], ckvb) + vone_ref[...]).astype(vt_ref.dtype)
    for h in range(AH):
        hs = slice(h * LANE, (h + 1) * LANE)
        qb = q[:, hs]
        rinv = lax.rsqrt(jnp.sum(qb * qb, axis=-1, keepdims=True) / A_QK + EPS)
        qh = (qb * rinv * gq_ref[:, hs] * cosv + qr[:, hs] * rinv * gqr_ref[:, hs] * sinv)
        kraw = kn[:, hs] + zkr
        ks = lax.rsqrt(jnp.sum(kraw * kraw, axis=-1, keepdims=True) / A_QK + EPS)
        ksc = jnp.where(lane == h, ks, ksc)
        if sample:
            qa_ref[:, hs] = qh * scale
            qn = (qh * gkn_ref[...]).astype(BF16)
            ql_ref[:, h * A_KVR:(h + 1) * A_KVR] = _dot(qn, wukt_ref[h]) * scale
        else:
            qa_ref[:, hs] = (qh * scale).astype(qa_ref.dtype)
            ka_ref[:, hs] = ((kn[:, hs] * gkn_ref[...] + kpe) * ks).astype(ka_ref.dtype)
    ksc_ref[...] = ksc[:, 0:AH]


def _mla_prep(zc, small, cos_t, sin_t, lw, *, tm, rows_per_pos, sample):
    m = zc.shape[0]
    if rows_per_pos is None:
        pos_spec = pl.BlockSpec((tm, LANE), lambda i: (i, 0))
    else:
        tpb = rows_per_pos // tm
        pos_spec = pl.BlockSpec((tm, LANE), lambda i: (i % tpb, 0))
    row = lambda w: pl.BlockSpec((tm, w), lambda i: (i, 0))
    out_shape = [
        jax.ShapeDtypeStruct((m, A_KVR), F32),
        jax.ShapeDtypeStruct((m, A_ROPE), F32),
        jax.ShapeDtypeStruct((m, AH), F32),
    ]
    out_specs = [row(A_KVR), row(A_ROPE), row(AH)]
    if sample:
        out_shape += [jax.ShapeDtypeStruct((m, AH * LANE), F32), jax.ShapeDtypeStruct((m, AH * A_KVR), F32)]
        out_specs += [row(AH * LANE), row(AH * A_KVR)]
    else:
        n_seq = m // rows_per_pos
        out_shape += [jax.ShapeDtypeStruct((m, AH * LANE), BF16)] * 2
        out_shape += [jax.ShapeDtypeStruct((n_seq * AH * LANE, rows_per_pos), BF16)]
        out_specs += [row(AH * LANE)] * 2
        out_specs += [pl.BlockSpec((AH * LANE, tm), lambda i: (i // tpb, i % tpb))]
    names =("wq", "wqr", "gq", "gqr", "wuk", "wuv", "vone", "gcq", "gckv", "gkn", "gkp", "wukt")
    ws = [lw[n] for n in names]
    return pl.pallas_call(
        functools.partial(_mla_kernel, sample=sample),
        out_shape=out_shape,
        grid=(m // tm,),
        in_specs=[row(512), row(LANE), pos_spec, pos_spec] + [_const_spec(w.shape) for w in ws],
        out_specs=out_specs,
        compiler_params=_cparams(("arbitrary",)),
        name="mla_prep",
    )(zc, small, cos_t, sin_t, *ws)


def _split3(x):
    hi = x.astype(BF16)
    r1 = x - hi.astype(F32)
    mid = r1.astype(BF16)
    lo = (r1 - mid.astype(F32)).astype(BF16)
    return hi, mid, lo


def _mlstm_kernel(q_ref, k_ref, v_ref, og_ref, small_ref, ghb_ref, c0_ref, n0_ref, m0_ref,
                  h_ref, c_ref, n_ref, m_ref, cnt_sc, *, n_chunks):
    n_seq, L = q_ref.shape[0], q_ref.shape[1]
    first = pl.program_id(1) == 0
    final = pl.program_id(1) == n_chunks - 1

    @pl.when(first)
    def _():
        m_ref[...] = m0_ref[...]
        for i in range(n_seq):
            for h in range(LH):
                cnt_sc[i, h, 0:LDV, :] = c0_ref[i, h].T
                cnt_sc[i, h, LDV:2 * LDV, :] = jnp.broadcast_to(n0_ref[i, h:h + 1, :], (LDV, LDK))

    s_idx = lax.broadcasted_iota(jnp.int32, (L, L), 0)
    t_idx = lax.broadcasted_iota(jnp.int32, (L, L), 1)
    upto = s_idx <= t_idx
    tril = jnp.where(t_idx <= s_idx, 1.0, 0.0).astype(BF16)
    lane = lax.broadcasted_iota(jnp.int32, (1, LANE), 1)
    ones = jnp.ones((L, LDV), BF16)
    for i in range(n_seq):
        _mlstm_chunk(i, q_ref, k_ref, v_ref, og_ref, small_ref, ghb_ref, h_ref, m_ref, cnt_sc,
                     upto, tril, lane, ones)

    @pl.when(final)
    def _():
        for i in range(n_seq):
            for h in range(LH):
                c_ref[i, h] = cnt_sc[i, h, 0:LDV, :].T
                n_ref[i, h:h + 1, :] = cnt_sc[i, h, LDV:LDV + 1, :]


def _mlstm_chunk(i, q_ref, k_ref, v_ref, og_ref, small_ref, ghb_ref, h_ref, m_ref, cnt_sc, upto, tril, lane, ones):
    L = q_ref.shape[1]
    small = small_ref[i]
    m_all = m_ref[i]
    m_out = m_all
    for h in range(LH):
        hs = slice(h * LDK, (h + 1) * LDK)
        zf = small[:, SM_F + h:SM_F + h + 1]
        lf = jnp.broadcast_to(jnp.minimum(zf, 0.0) - jnp.log1p(jnp.exp(-jnp.abs(zf))), (L, LDK))
        ig = jnp.broadcast_to(small[:, SM_I + h:SM_I + h + 1], (L, LDK))
        b_row = jnp.sum(jnp.where(upto, lf[:, 0:L], 0.0), axis=0, keepdims=True)
        b_bc = functools.reduce(jnp.add, [_dot(tril, piece) for piece in _split3(lf)])
        g = ig - b_bc
        logw = jnp.where(upto, b_row + g[:, 0:L], NEG_INF)
        m_prev = m_all[:, h:h + 1]
        m_inter = m_prev + b_row
        m_t = jnp.maximum(m_inter, jnp.max(logw, axis=0, keepdims=True))
        qh = q_ref[i, :, hs]
        kh = k_ref[i, :, hs]
        v1 = jnp.concatenate([v_ref[i, :, hs], ones], axis=1)
        smat = _dot_nt(kh, qh) * jnp.exp(logw - m_t)
        a_inter = jnp.exp(m_inter - m_t)
        cnt = cnt_sc[i, h]
        nd = (lax.dot_general(v1, smat.astype(BF16), (((0,), (0,)), ((), ())), preferred_element_type=F32)
              + a_inter * _dot_nt(cnt.astype(BF16), qh))
        hv = nd[0:LDV] / jnp.maximum(jnp.abs(nd[LDV:2 * LDV]), jnp.exp(-m_t))
        hn = hv * lax.rsqrt(jnp.mean(hv * hv, axis=0, keepdims=True) + EPS) * ghb_ref[h, :, 0:L]
        h_ref[i, :, hs] = (og_ref[i, :, hs].astype(F32) * hn.T).astype(h_ref.dtype)
        m_new = m_t[:, L - 1:L]
        b_last = b_row[:, L - 1:L]
        decay = jnp.exp(m_prev + b_last - m_new)
        wk = (kh.astype(F32) * jnp.exp(g + (b_last - m_new))).astype(BF16)
        cnt_new = decay * cnt + lax.dot_general(v1, wk, (((0,), (0,)), ((), ())), preferred_element_type=F32)
        cnt_sc[i, h] = cnt_new
        m_out = jnp.where(lane == h, m_new, m_out)
    m_ref[i] = m_out


MLSTM_SEQS = 2


def _mlstm(q, k, v, og, small, gh, c0, n0, m0, *, n_chunks, layer):
    m = q.shape[0]
    b = c0.shape[1]
    L = m // (b * n_chunks)
    assert L <= LCHUNK and L % BF16_ROWS == 0
    ns = MLSTM_SEQS
    seq3 = lambda a: a.reshape(b, n_chunks * L, a.shape[-1])
    row = lambda w: pl.BlockSpec((ns, L, w), lambda i, c: (i, c, 0))
    st_c = pl.BlockSpec((ns, LH, LDK, LDV), lambda i, c: (i, 0, 0, 0))
    st_n = pl.BlockSpec((ns, LH, LDK), lambda i, c: (i, 0, 0))
    st_m = pl.BlockSpec((ns, 1, LANE), lambda i, c: (i, 0, 0))
    in_c = pl.BlockSpec((None, ns, LH, LDK, LDV), lambda i, c: (layer, i, 0, 0, 0))
    in_n = pl.BlockSpec((None, ns, LH, LDK), lambda i, c: (layer, i, 0, 0))
    h3, c_f, n_f, m_f = pl.pallas_call(
        functools.partial(_mlstm_kernel, n_chunks=n_chunks),
        out_shape=[
            jax.ShapeDtypeStruct((b, n_chunks * L, 512), BF16),
            jax.ShapeDtypeStruct((b, LH, LDK, LDV), F32),
            jax.ShapeDtypeStruct((b, LH, LDK), F32),
            jax.ShapeDtypeStruct((b, 1, LANE), F32),
        ],
        grid=(b // ns, n_chunks),
        in_specs=[row(512), row(512), row(512), row(512), row(LANE), _const_spec(gh.shape), in_c, in_n, st_m],
        out_specs=[row(512), st_c, st_n, st_m],
        scratch_shapes=[pltpu.VMEM((ns, LH, 2 * LDV, LDK), F32)],
        compiler_params=_cparams(("arbitrary", "arbitrary")),
        name="mlstm",
    )(seq3(q), seq3(k), seq3(v), seq3(og), seq3(small), gh, c0, n0, m0)
    return h3.reshape(m, 512), c_f, n_f, m_f


def _attn_kernel(q_ref, k_ref, vt_ref, o_ref, p_sc, *, tq):
    t = q_ref.shape[0]
    key = lax.broadcasted_iota(jnp.int32, (tq, tq), 0)
    qry = lax.broadcasted_iota(jnp.int32, (tq, tq), 1)
    for i in range(t // tq):
        r0, r1 = i * tq, (i + 1) * tq
        st = _dot_nt(k_ref[0:r1, :], q_ref[r0:r1, :])
        st_d = jnp.where(key <= qry, st[r0:], NEG_INF)
        m = jnp.max(st_d, axis=0, keepdims=True)
        if i > 0:
            m = jnp.maximum(m, jnp.max(st[:r0], axis=0, keepdims=True))
            p_sc[0:r0] = jnp.exp(st[:r0] - m).astype(BF16)
        p_sc[r0:r1] = jnp.exp(st_d - m).astype(BF16)
        ot = _dot(vt_ref[:, 0:r1], p_sc[0:r1])
        o_ref[r0:r1, :] = (ot / ot[A_V:A_V + 1, :]).T.astype(o_ref.dtype)


def _attn_prompt(qa, ka, vt, *, b, t, tq):
    blk = pl.BlockSpec((t, LANE), lambda i, h: (i, h))
    return pl.pallas_call(
        functools.partial(_attn_kernel, tq=tq),
        out_shape=jax.ShapeDtypeStruct(qa.shape, BF16),
        grid=(b, AH),
        in_specs=[blk, blk, pl.BlockSpec((LANE, t), lambda i, h: (i * AH + h, 0))],
        out_specs=blk,
        scratch_shapes=[pltpu.VMEM((t, tq), BF16)],
        compiler_params=_cparams(("arbitrary", "arbitrary")),
        name="attn_prompt",
    )(qa, ka, vt)


def _paged_kernel(pt_ref, ql_ref, qp_ref, cn_ref, kpn_ref, ksn_ref, ckv_hbm, kpe_hbm, ks_hbm,
                  o_ref, cbuf, kpbuf, ksbuf, sems, m_sc, l_sc, acc_sc, *, layer, n_per_step, n_chains, t_new):
    j = pl.program_id(1)
    steps = pl.num_programs(1)
    step = pl.program_id(0) * steps + j
    last = pl.num_programs(0) * steps - 1
    slot = step % PAGED_SLOTS
    ql = ql_ref[...]
    qp = qp_ref[...]
    nr = ql.shape[0]

    def page_copies(step_, n):
        slot_ = step_ % PAGED_SLOTS
        pg = pt_ref[jnp.minimum(step_, last) * n_per_step + n]
        return (pltpu.make_async_copy(ckv_hbm.at[layer, pg], cbuf.at[slot_, n], sems.at[slot_, 0]),
                pltpu.make_async_copy(kpe_hbm.at[layer, pg], kpbuf.at[slot_, n], sems.at[slot_, 1]),
                pltpu.make_async_copy(ks_hbm.at[layer, pg], ksbuf.at[slot_, n], sems.at[slot_, 2]))

    @pl.when(step == 0)
    def _():
        for ahead in range(PAGED_SLOTS - 1):
            for n in range(n_per_step):
                for cp in page_copies(ahead, n):
                    cp.start()

    for n in range(n_per_step):
        for cp in page_copies(step, n):
            cp.wait()

    @pl.when(j == 0)
    def _():
        cn = cn_ref[...].astype(BF16)
        s = (_dot_nt(ql, cn) + _dot(qp, kpn_ref[...].astype(BF16))) * ksn_ref[...]
        nk = cn.shape[0]
        r_t = lax.broadcasted_iota(jnp.int32, (nr, nk), 0) // AH
        key = lax.broadcasted_iota(jnp.int32, (nr, nk), 1)
        s = jnp.where((key <= r_t) & (key < t_new), s, NEG_INF)
        m0 = jnp.max(s, axis=-1, keepdims=True)
        p = jnp.exp(s - m0)
        m_sc[0] = m0
        l_sc[0] = jnp.sum(p, axis=-1, keepdims=True)
        acc_sc[0] = _dot(p.astype(BF16), cn)
        for c in range(1, n_chains):
            m_sc[c] = m0
            l_sc[c] = jnp.zeros_like(m0)
            acc_sc[c] = jnp.zeros((nr, A_KVR), F32)

    per_chain = n_per_step // n_chains
    cbs, ss = [], []
    for n in range(n_per_step):
        for cp in page_copies(step + PAGED_SLOTS - 1, n):
            cp.start()
        cb = cbuf[slot, n].astype(BF16)
        kpt = kpbuf[slot, n].astype(BF16)
        ks = ksbuf[slot, n]
        kst = jnp.concatenate([ks] * t_new, axis=0)
        cbs.append(cb)
        ss.append((_dot_nt(ql, cb) + _dot(qp, kpt)) * kst)
    finals = []
    for c in range(n_chains):
        sc = ss[c * per_chain:(c + 1) * per_chain]
        cc = cbs[c * per_chain:(c + 1) * per_chain]
        m_i = m_sc[c]
        m_new = jnp.maximum(m_i, jnp.max(functools.reduce(jnp.maximum, sc), axis=-1, keepdims=True))
        alpha = jnp.exp(m_i - m_new)
        ps = [jnp.exp(s - m_new) for s in sc]
        pv = functools.reduce(jnp.add, [_dot(p.astype(BF16), cb) for p, cb in zip(ps, cc)])
        l_i = alpha * l_sc[c] + jnp.sum(functools.reduce(jnp.add, ps), axis=-1, keepdims=True)
        acc = alpha * acc_sc[c] + pv
        m_sc[c] = m_new
        l_sc[c] = l_i
        acc_sc[c] = acc
        finals.append((m_new, l_i, acc))

    @pl.when(j == steps - 1)
    def _():
        m_f = functools.reduce(jnp.maximum, [f[0] for f in finals])
        ws = [jnp.exp(f[0] - m_f) for f in finals]
        l_f = functools.reduce(jnp.add, [w * f[1] for w, f in zip(ws, finals)])
        acc_f = functools.reduce(jnp.add, [w * f[2] for w, f in zip(ws, finals)])
        o_ref[...] = acc_f / l_f

    @pl.when(step == last)
    def _():
        for ahead in range(1, PAGED_SLOTS):
            for n in range(n_per_step):
                for cp in page_copies(last + ahead, n):
                    cp.wait()


def _attn_paged(page_table, qlat, qp, c_new, kp_new_t, ks_new_t, cache_ckv, cache_kpe_t, cache_ks_t, *, layer, t_new,
                n_per_step, n_chains):
    bs, nr, _ = qlat.shape
    n_pages = page_table.shape[1]
    nk = c_new.shape[1]
    steps = n_pages // n_per_step
    pt_flat = page_table.reshape(-1)

    per_b = lambda s1, s2: pl.BlockSpec((None, s1, s2), lambda b, j, pt: (b, 0, 0))
    hbm = pl.BlockSpec(memory_space=pl.ANY)
    grid_spec = pltpu.PrefetchScalarGridSpec(
        num_scalar_prefetch=1,
        grid=(bs, steps),
        in_specs=[per_b(nr, A_KVR), per_b(nr, A_ROPE), per_b(nk, A_KVR), per_b(A_ROPE, nk), per_b(nr, nk),
                  hbm, hbm, hbm],
        out_specs=per_b(nr, A_KVR),
        scratch_shapes=[
            pltpu.VMEM((PAGED_SLOTS, n_per_step, PAGE, A_KVR), F32),
            pltpu.VMEM((PAGED_SLOTS, n_per_step, A_ROPE, PAGE), F32),
            pltpu.VMEM((PAGED_SLOTS, n_per_step, AH, PAGE), F32),
            pltpu.SemaphoreType.DMA((PAGED_SLOTS, 3)),
            pltpu.VMEM((n_chains, nr, 1), F32), pltpu.VMEM((n_chains, nr, 1), F32),
            pltpu.VMEM((n_chains, nr, A_KVR), F32),
        ],
    )
    return pl.pallas_call(
        functools.partial(_paged_kernel, layer=layer, n_per_step=n_per_step, n_chains=n_chains, t_new=t_new),
        out_shape=jax.ShapeDtypeStruct((bs, nr, A_KVR), F32),
        grid_spec=grid_spec,
        compiler_params=_cparams(("arbitrary", "arbitrary")),
        name="attn_paged",
    )(pt_flat, qlat, qp, c_new, kp_new_t, ks_new_t, cache_ckv, cache_kpe_t, cache_ks_t)


def _uv_kernel(x_ref, w_ref, o_ref):
    o_ref[...] = _dot(x_ref[...].astype(BF16), w_ref[...]).astype(o_ref.dtype)


def _uv_proj(olat_h, wuvp):
    _, m, _ = olat_h.shape
    return pl.pallas_call(
        _uv_kernel,
        out_shape=jax.ShapeDtypeStruct((m, AH * LANE), BF16),
        grid=(AH,),
        in_specs=[pl.BlockSpec((None, m, A_KVR), lambda h: (h, 0, 0)),
                  pl.BlockSpec((None, A_KVR, LANE), lambda h: (h, 0, 0))],
        out_specs=pl.BlockSpec((m, LANE), lambda h: (0, h)),
        compiler_params=_cparams(("arbitrary",)),
        name="uv_proj",
    )(olat_h, wuvp)


def _merge_kernel(x_ref, g1_ref, oa_ref, ob_ref, oc_ref, gates_ref, wa_ref, wb_ref, wc_ref, wo_ref, o_ref):
    mix = gates_ref[:, 0:D].astype(F32) * _dot(oa_ref[...], wa_ref[...])
    mix += gates_ref[:, D:2 * D].astype(F32) * _dot(ob_ref[...], wb_ref[...])
    mix += gates_ref[:, 2 * D:3 * D].astype(F32) * _dot(oc_ref[...], wc_ref[...])
    o_ref[...] = x_ref[...] + g1_ref[...] * _dot(mix.astype(BF16), wo_ref[...])


def _mod_spec(tm, rows_per_mod):
    if rows_per_mod is None:
        return pl.BlockSpec((tm, D), lambda i: (i, 0))
    tpb = rows_per_mod // tm
    return pl.BlockSpec((None, 1, D), lambda i: (i // tpb, 0, 0))


def _merge(x2, g1, oa, ob, oc, gates, wa, wb, wc, wo, *, tm, rows_per_mod):
    m = x2.shape[0]
    row = lambda w: pl.BlockSpec((tm, w), lambda i: (i, 0))
    return pl.pallas_call(
        _merge_kernel,
        out_shape=jax.ShapeDtypeStruct((m, D), F32),
        grid=(m // tm,),
        in_specs=[row(D), _mod_spec(tm, rows_per_mod), row(512), row(512), row(AH * LANE), row(3 * D),
                  _const_spec(wa.shape), _const_spec(wb.shape), _const_spec(wc.shape), _const_spec(wo.shape)],
        out_specs=row(D),
        compiler_params=_cparams(("arbitrary",)),
        name="merge",
    )(x2, g1, oa, ob, oc, gates, wa, wb, wc, wo)


def _moe_kernel(x_ref, sc_ref, sh_ref, g2_ref, gn_ref, wr1_ref, wr2_ref, br_ref, wg_ref, wu_ref, wd_ref, o_ref):
    tm = x_ref.shape[0]
    x = x_ref[...]
    h2 = _rms(x, gn_ref[...]) * (1.0 + sc_ref[...]) + sh_ref[...]
    hi = h2.astype(BF16)
    lo = (h2 - hi.astype(F32)).astype(BF16)
    r = _dot(hi, wr1_ref[...]) + _dot(lo, wr2_ref[...])
    rt = r.T
    logits = rt[0:32] + rt[32:64]
    scores = _sigmoid(logits)
    sel = scores + br_ref[...]
    e = [sel[8 * p:8 * p + 8] for p in range(GRP_E)]
    sc = [scores[8 * p:8 * p + 8] for p in range(GRP_E)]
    a, b = jnp.maximum(e[0], e[1]), jnp.minimum(e[0], e[1])
    c, d = jnp.maximum(e[2], e[3]), jnp.minimum(e[2], e[3])
    top1 = jnp.maximum(a, c)
    top2 = jnp.maximum(jnp.minimum(a, c), jnp.maximum(b, d))
    rowi = lax.broadcasted_iota(jnp.int32, (8, tm), 0).astype(F32)
    gs = jnp.where(rowi < N_GRP, top1 + top2, NEG_INF)
    gmax = jnp.max(gs, axis=0, keepdims=True)
    gidx = jnp.min(jnp.where(gs == gmax, rowi, 8.0), axis=0, keepdims=True)
    oh = rowi == gidx
    s = [jnp.sum(jnp.where(oh, e[p], 0.0), axis=0, keepdims=True) for p in range(GRP_E)]
    w = [jnp.sum(jnp.where(oh, sc[p], 0.0), axis=0, keepdims=True) for p in range(GRP_E)]

    def first_argmax(vals):
        best = jnp.maximum(jnp.maximum(vals[0], vals[1]), jnp.maximum(vals[2], vals[3]))
        return jnp.where(vals[0] == best, 0, jnp.where(vals[1] == best, 1, jnp.where(vals[2] == best, 2, 3)))

    i1 = first_argmax(s)
    i2 = first_argmax([jnp.where(i1 == p, NEG_INF, s[p]) for p in range(GRP_E)])
    w1 = sum(jnp.where(i1 == p, w[p], 0.0) for p in range(GRP_E))
    w2 = sum(jnp.where(i2 == p, w[p], 0.0) for p in range(GRP_E))
    den = w1 + w2
    w1 = w1 / den
    w2 = w2 / den
    comb = [jnp.where(oh, jnp.where(i1 == p, w1, 0.0) + jnp.where(i2 == p, w2, 0.0), 0.0) for p in range(GRP_E)]
    comb_t = jnp.concatenate(comb + [jnp.zeros((LANE - 8 * GRP_E, tm), F32)], axis=0)
    combine = comb_t.T

    acc = jnp.zeros((tm, D), F32)
    for g in range(N_GRP):
        acts = []
        for p in range(GRP_E):
            ex = g * GRP_E + p
            hg = _dot(hi, wg_ref[ex])
            hu = _dot(hi, wu_ref[ex])
            cw = combine[:, 8 * p + g:8 * p + g + 1]
            acts.append((hg * _sigmoid(hg) * hu * cw).astype(BF16))
        acc += _dot(jnp.concatenate(acts, axis=1), wd_ref[g])
    o_ref[...] = x + g2_ref[...] * acc


def _moe(x2, sc, sh, g2, gn, wr1, wr2, br, wg, wu, wd, *, tm, rows_per_mod):
    m = x2.shape[0]
    row = lambda w: pl.BlockSpec((tm, w), lambda i: (i, 0))
    ms = _mod_spec(tm, rows_per_mod)
    return pl.pallas_call(
        _moe_kernel,
        out_shape=jax.ShapeDtypeStruct((m, D), F32),
        grid=(m // tm,),
        in_specs=[row(D), ms, ms, ms, _const_spec(gn.shape), _const_spec(wr1.shape), _const_spec(wr2.shape),
                  _const_spec(br.shape), _const_spec(wg.shape), _const_spec(wu.shape), _const_spec(wd.shape)],
        out_specs=row(D),
        compiler_params=_cparams(("arbitrary",)),
        name="moe",
    )(x2, sc, sh, g2, gn, wr1, wr2, br, wg, wu, wd)


def _head_block(nope, r1, r2):
    pad = jnp.zeros(nope.shape[:-1] + (LANE - A_QK,), nope.dtype)
    blk = jnp.concatenate([nope, r1, r2, pad], axis=-1)
    return blk.reshape(blk.shape[:-2] + (AH * LANE,))


def _prep_layer(l, w_in, b_in, w_gate, b_gate, g_norm1, g_norm2, g_sgu, g_mlstm, g_cq, w_q_up, g_ckv, w_uk, w_uv,
                g_qn, g_kn, w_br, w_o, w_exp_gate, w_exp_up, w_exp_down):
    half = A_ROPE // 2
    wi, bi = w_in[l], b_in[l]

    def cat_cols(a, gate):
        z = lambda n: jnp.zeros(a.shape[:-1] + (n,), a.dtype)
        small = jnp.concatenate([z(SM_KR), a[..., 3592:3624], a[..., 3072:3080], z(LANE - SM_F - LH)], axis=-1)
        return jnp.concatenate([a[..., :3072], a[..., 3080:3592], small, gate], axis=-1)

    lw = {}
    lw["wcat"] = cat_cols(wi, w_gate[l]).astype(BF16)
    lw["bcat"] = cat_cols(bi, b_gate[l]).reshape(1, N_CAT)
    lw["g1"] = g_norm1[l].reshape(1, D)
    lw["g2"] = g_norm2[l].reshape(1, D)
    lw["gsgu"] = g_sgu[l].reshape(1, SGU_W)
    lw["gh"] = jnp.broadcast_to(g_mlstm[l][:, :, None], (LH, LDV, LCHUNK))
    wq = w_q_up[l]
    nope, r1, r2 = wq[..., :A_NOPE], wq[..., A_NOPE:A_NOPE + half], wq[..., A_NOPE + half:]
    lw["wq"] = _head_block(nope, r1, r2).astype(BF16)
    lw["wqr"] = _head_block(jnp.zeros_like(nope), r2, r1).astype(BF16)
    gq = jnp.broadcast_to(g_qn[l], (AH, A_QK))
    gn_, g1_, g2_ = gq[:, :A_NOPE], gq[:, A_NOPE:A_NOPE + half], gq[:, A_NOPE + half:]
    lw["gq"] = _head_block(gn_, g1_, g2_).reshape(1, AH * LANE)
    lw["gqr"] = _head_block(jnp.zeros_like(gn_), g2_, g1_).reshape(1, AH * LANE)
    zr = jnp.zeros((A_KVR, AH, half), F32)
    lw["wuk"] = _head_block(w_uk[l], zr, zr).astype(BF16)
    uv = jnp.concatenate([w_uv[l], jnp.zeros((A_KVR, AH, LANE - A_V), F32)], axis=-1)
    lw["wuv"] = uv.reshape(A_KVR, AH * LANE).T.astype(BF16)
    lw["wuvp"] = jnp.transpose(uv, (1, 0, 2)).astype(BF16)
    lw["vone"] = jnp.tile((jnp.arange(LANE) == A_V).astype(F32), AH).reshape(AH * LANE, 1)
    lw["gcq"] = g_cq[l].reshape(1, A_QR)
    lw["gckv"] = g_ckv[l].reshape(1, A_KVR)
    gk = g_kn[l]
    lw["gkn"] = jnp.concatenate([gk[:A_NOPE], jnp.zeros((LANE - A_NOPE,), F32)]).reshape(1, LANE)
    lw["gkp"] = jnp.concatenate([jnp.zeros((SM_KR,), F32), gk[A_NOPE:], jnp.zeros((LANE - A_QK,), F32)]).reshape(1, LANE)
    ukt = jnp.transpose(w_uk[l], (1, 2, 0))
    lw["wukt"] = jnp.concatenate([ukt, jnp.zeros((AH, LANE - A_NOPE, A_KVR), F32)], axis=1).astype(BF16)
    lw["wa"] = w_br[l, 0].astype(BF16)
    lw["wb"] = w_br[l, 1].astype(BF16)
    wc = w_br[l, 2].reshape(AH, A_V, D)
    lw["wc"] = jnp.concatenate([wc, jnp.zeros((AH, LANE - A_V, D), F32)], axis=1).reshape(AH * LANE, D).astype(BF16)
    lw["wo"] = w_o[l].astype(BF16)
    lw["wg"] = w_exp_gate[l].astype(BF16)
    lw["wu"] = w_exp_up[l].astype(BF16)
    lw["wd"] = w_exp_down[l].reshape(N_GRP, GRP_E * D_EXP, D).astype(BF16)
    return lw


def _prep_router(w_router, b_router):
    wr = w_router.reshape(D, N_GRP, GRP_E).transpose(0, 2, 1)
    wr = jnp.concatenate([wr, jnp.zeros((D, GRP_E, 8 - N_GRP), F32)], axis=-1).reshape(D, 8 * GRP_E)
    hi = wr.astype(BF16)
    lo = (wr - hi.astype(F32)).astype(BF16)
    z = lambda n: jnp.zeros((D, n), BF16)
    wr1 = jnp.concatenate([hi, lo, z(LANE - 64)], axis=1)
    wr2 = jnp.concatenate([hi, z(LANE - 32)], axis=1)
    br = b_router.reshape(N_GRP, GRP_E).T
    br = jnp.concatenate([br, jnp.zeros((GRP_E, 8 - N_GRP), F32)], axis=-1).reshape(8 * GRP_E, 1)
    return wr1, wr2, br


def _rope_tables(pos):
    half = A_ROPE // 2
    freqs = ROPE_BASE ** (-jnp.arange(half, dtype=F32) / half)
    ang = pos[:, None] * freqs[None, :]
    cos, sin = jnp.cos(ang), jnp.sin(ang)
    n = pos.shape[0]
    cos_t = jnp.concatenate([jnp.ones((n, A_NOPE), F32), cos, cos, jnp.zeros((n, LANE - A_QK), F32)], axis=1)
    sin_t = jnp.concatenate([jnp.zeros((n, A_NOPE), F32), -sin, sin, jnp.zeros((n, LANE - A_QK), F32)], axis=1)
    return cos_t, sin_t


def _mix_weights(w_spatial_l, b_spatial_l, chunk, reps):
    w = jnp.where(jnp.tril(jnp.ones((chunk, chunk), dtype=bool))[None], w_spatial_l[:, :chunk, :chunk], 0.0)
    if reps > 1:
        idx = jnp.arange(reps * chunk)
        pos = jax.nn.one_hot(idx % chunk, chunk, dtype=F32)
        same = (idx[:, None] // chunk) == (idx[None, :] // chunk)
        w = jnp.where(same[None], jnp.einsum("rt,gts,cs->grc", pos, w, pos, precision=lax.Precision.HIGHEST), 0.0)
    bias = jnp.tile(b_spatial_l[:, :chunk].T, (reps, 1))
    bias = jnp.repeat(bias, SGU_C, axis=1)
    return w.astype(BF16), bias


def _layer_prompt(x2, mod, lw, router, tables, wmix, bmix, b, t):
    m = b * t
    tm = 512
    sh1, sc1, g1, sh2, sc2, g2 = [a.reshape(b, 1, D) for a in jnp.split(mod, 6, axis=-1)]
    oa, q, k, v, og, zc, small, gates = _proj(x2, sc1, sh1, lw["wcat"], lw["bcat"], lw["g1"], lw["gsgu"], wmix, bmix,
                                              tm=tm, rows_per_mod=t, emit_v=False)
    ckv, kpe, ksc, qa, ka, va = _mla_prep(zc, small, tables[0], tables[1], lw, tm=tm, rows_per_pos=t, sample=False)
    n_chunks = t // LCHUNK
    c0 = jnp.zeros((b, LH, LDK, LDV), F32)
    n0 = jnp.zeros((b, LH, LDK), F32)
    m0 = jnp.zeros((b, 1, LANE), F32)
    ob, c_f, n_f, m_f = _mlstm(q, k, v, og, small, lw["gh"], c0[None], n0[None], m0, n_chunks=n_chunks, layer=0)
    oc = _attn_prompt(qa, ka, va, b=b, t=t, tq=min(t, 256))
    x1 = _merge(x2, g1, oa, ob, oc, gates, lw["wa"], lw["wb"], lw["wc"], lw["wo"], tm=tm, rows_per_mod=t)
    x_out = _moe(x1, sc2, sh2, g2, lw["g2"], *router, lw["wg"], lw["wu"], lw["wd"], tm=tm, rows_per_mod=t)
    state = (ckv.reshape(b, t, A_KVR), kpe.reshape(b, t, A_ROPE), ksc.reshape(b, t, AH), c_f, n_f, m_f[:, 0, :LH])
    return x_out, state


def _layer_sample(x2, mod, lw, router, tables, wmix, bmix, b, t, st_c, st_n, st_m, caches, page_table, layer):
    m = b * t
    tm = m
    mods = [jnp.repeat(a, t, axis=0) for a in jnp.split(mod, 6, axis=-1)]
    sh1, sc1, g1, sh2, sc2, g2 = mods
    oa, q, k, v, og, zc, small, gates, v_sgu = _proj(x2, sc1, sh1, lw["wcat"], lw["bcat"], lw["g1"], lw["gsgu"], wmix,
                                                     bmix, tm=tm, rows_per_mod=None, emit_v=True)
    ckv, kpe, ksc, qa, qlat = _mla_prep(zc, small, tables[0], tables[1], lw, tm=tm, rows_per_pos=None, sample=True)

    lpad = -(-t // BF16_ROWS) * BF16_ROWS

    def pad_rows(a, fill=None):
        a3 = a.reshape(b, t, a.shape[-1])
        if fill is None:
            padv = jnp.zeros((b, lpad - t, a.shape[-1]), a.dtype)
        else:
            padv = jnp.broadcast_to(fill, (b, lpad - t, a.shape[-1])).astype(a.dtype)
        return jnp.concatenate([a3, padv], axis=1).reshape(b * lpad, a.shape[-1])

    lane = jnp.arange(LANE)
    inert = jnp.where((lane >= SM_I) & (lane < SM_I + LH), -1e30, jnp.where((lane >= SM_F) & (lane < SM_F + LH), 1e30, 0.0))
    m0 = jnp.concatenate([st_m, jnp.zeros((b, LANE - LH), F32)], axis=1).reshape(b, 1, LANE)
    ob_p, c_f, n_f, m_f = _mlstm(pad_rows(q), pad_rows(k), pad_rows(v), pad_rows(og), pad_rows(small, inert.astype(F32)),
                                 lw["gh"], st_c, st_n, m0, n_chunks=1, layer=layer)
    ob = ob_p.reshape(b, lpad, 512)[:, :t].reshape(m, 512)

    nr = t * AH
    nk = 8
    ql3 = qlat.reshape(b, nr, A_KVR).astype(BF16)
    qp3 = qa.reshape(b, t, AH, LANE)[..., A_NOPE:A_QK].reshape(b, nr, A_ROPE).astype(BF16)
    padk = lambda a: jnp.concatenate([a.reshape(b, t, -1), jnp.zeros((b, nk - t, a.shape[-1]), F32)], axis=1)
    ks_t = jnp.transpose(padk(ksc), (0, 2, 1))
    ks_t = jnp.tile(ks_t, (1, t, 1))
    kp_t = jnp.transpose(padk(kpe), (0, 2, 1))
    olat = _attn_paged(page_table, ql3, qp3, padk(ckv), kp_t, ks_t, *caches, layer=layer, t_new=t, n_per_step=32,
                       n_chains=4)
    olat_h = jnp.transpose(olat.reshape(b, t, AH, A_KVR), (2, 0, 1, 3)).reshape(AH, m, A_KVR)
    oc = _uv_proj(olat_h, lw["wuvp"])

    x1 = _merge(x2, g1, oa, ob, oc, gates, lw["wa"], lw["wb"], lw["wc"], lw["wo"], tm=tm, rows_per_mod=None)
    x_out = _moe(x1, sc2, sh2, g2, lw["g2"], *router, lw["wg"], lw["wu"], lw["wd"], tm=tm, rows_per_mod=None)
    state = (ckv.reshape(b, t, A_KVR), kpe.reshape(b, t, A_ROPE), ksc.reshape(b, t, AH), c_f, n_f, m_f[:, 0, :LH],
             v_sgu.reshape(b, t, SGU_W))
    return x_out, state


def kernel(x_prompt, x_sample, c_prompt, c_sample, cache_ckv, cache_kpe, cache_kscale, page_table, state_C, state_n, state_m, w_ada, b_ada, g_norm1, g_norm2, w_in, b_in, g_sgu, w_spatial, b_spatial, g_mlstm, g_cq, w_q_up, g_ckv, w_uk, w_uv, g_qn, g_kn, w_br, w_gate, b_gate, w_o, w_router, b_router, w_exp_gate, w_exp_up, w_exp_down):
    bp, tp, _ = x_prompt.shape
    bs, ts, _ = x_sample.shape
    depth = w_in.shape[0]
    past_len = page_table.shape[1] * PAGE

    mod_all = _ada(jnp.concatenate([c_prompt, c_sample], axis=0), w_ada, b_ada)
    router = _prep_router(w_router, b_router)
    tab_p = _rope_tables(jnp.arange(tp, dtype=F32))
    pos_s = jnp.tile(jnp.arange(ts, dtype=F32) + past_len, bs)
    tab_s = _rope_tables(pos_s)
    caches = (cache_ckv, jnp.swapaxes(cache_kpe, 2, 3), jnp.swapaxes(cache_kscale, 2, 3))

    xp = x_prompt.reshape(bp * tp, D)
    xs = x_sample.reshape(bs * ts, D)
    st_p, st_s = [], []
    for l in range(depth):
        lw = _prep_layer(l, w_in, b_in, w_gate, b_gate, g_norm1, g_norm2, g_sgu, g_mlstm, g_cq, w_q_up, g_ckv, w_uk,
                         w_uv, g_qn, g_kn, w_br, w_o, w_exp_gate, w_exp_up, w_exp_down)
        chunk_p = min(tp, SGU_C)
        wmix_p, bmix_p = _mix_weights(w_spatial[l], b_spatial[l], chunk_p, 1)
        wmix_s, bmix_s = _mix_weights(w_spatial[l], b_spatial[l], ts, bs)
        xp, sp = _layer_prompt(xp, mod_all[l, :bp], lw, router, tab_p, wmix_p, bmix_p, bp, tp)
        xs, ss = _layer_sample(xs, mod_all[l, bp:], lw, router, tab_s, wmix_s, bmix_s, bs, ts, state_C, state_n,
                               state_m[l], caches, page_table, l)
        st_p.append(sp)
        st_s.append(ss)
    stack = lambda sts, i: jnp.stack([s[i] for s in sts])
    return (xp.reshape(bp, tp, D), xs.reshape(bs, ts, D),
            stack(st_p, 0), stack(st_p, 1), stack(st_p, 2), stack(st_p, 3), stack(st_p, 4), stack(st_p, 5),
            stack(st_s, 0), stack(st_s, 1), stack(st_s, 2), stack(st_s, 3), stack(st_s, 4), stack(st_s, 5),
            stack(st_s, 6))
```

```python
import functools
import math

import jax
import jax.numpy as jnp
from jax import lax
from jax.experimental import pallas as pl
from jax.experimental.pallas import tpu as pltpu

F32 = jnp.float32
BF16 = jnp.bfloat16
NEG_INF = float("-inf")

EPS = 1e-6
D = 1024
SGU_G = 4
SGU_C = 128
SGU_W = SGU_G * SGU_C
LH = 4
LDK = 128
LDV = 128
LCHUNK = 128
AH = 8
A_QR = 256
A_KVR = 256
A_NOPE = 64
A_ROPE = 32
A_QK = A_NOPE + A_ROPE
A_V = 64
ROPE_BASE = 10000.0
PAGE = 128
N_EXP = 16
N_GRP = 4
GRP_E = 4
D_EXP = 256

LANE = 128
BF16_ROWS = 16
PAGED_SLOTS = 3
C_ZU, C_ZV, C_ZQ, C_ZK, C_ZVV, C_ZO = 0, 512, 1024, 1536, 2048, 2560
C_ZC = 3072
C_SMALL = 3584
C_GATE = 3712
N_CAT = C_GATE + 3 * D
SM_KR, SM_I, SM_F = 64, 96, 100

VMEM_LIMIT = 56 * 1024 * 1024


def _cparams(sem):
    return pltpu.CompilerParams(dimension_semantics=sem, vmem_limit_bytes=VMEM_LIMIT)


def _const_spec(shape):
    nd = len(shape)
    return pl.BlockSpec(shape, lambda *_: (0,) * nd, pipeline_mode=pl.Buffered(1))


def _sigmoid(x):
    return 0.5 * jnp.tanh(0.5 * x) + 0.5


def _dot(a, b):
    return jnp.dot(a, b, preferred_element_type=F32)


def _dot_nt(a, b):
    return lax.dot_general(a, b, (((1,), (1,)), ((), ())), preferred_element_type=F32)


def _rms(x, g):
    return x * lax.rsqrt(jnp.mean(x * x, axis=-1, keepdims=True) + EPS) * g


def _ada_kernel(c_ref, w_ref, b_ref, o_ref):
    c = c_ref[...]
    s = (c * _sigmoid(c)).astype(BF16)
    o_ref[...] = _dot(s, w_ref[...].astype(BF16)) + b_ref[...]


def _ada(c_all, w_ada, b_ada):
    depth, _, n = w_ada.shape
    bc = c_all.shape[0]
    tn = 1536
    return pl.pallas_call(
        _ada_kernel,
        out_shape=jax.ShapeDtypeStruct((depth, bc, n), F32),
        grid=(depth, n // tn),
        in_specs=[
            pl.BlockSpec((bc, D), lambda l, j: (0, 0)),
            pl.BlockSpec((None, D, tn), lambda l, j: (l, 0, j)),
            pl.BlockSpec((None, 1, tn), lambda l, j: (l, 0, j)),
        ],
        out_specs=pl.BlockSpec((None, bc, tn), lambda l, j: (l, 0, j)),
        compiler_params=_cparams(("arbitrary", "arbitrary")),
        name="ada",
    )(c_all, w_ada, b_ada.reshape(depth, 1, n))


def _proj_kernel(x_ref, sc_ref, sh_ref, w_ref, b_ref, g1_ref, gsgu_ref, wmix_ref, bmix_ref,
                 oa_ref, q_ref, k_ref, v_ref, og_ref, zc_ref, small_ref, gates_ref, *rest, mix_rows, emit_v):
    tm = x_ref.shape[0]
    x = x_ref[...]
    h = _rms(x, g1_ref[...]) * (1.0 + sc_ref[...]) + sh_ref[...]
    hb = h.astype(BF16)

    def sec(a, b):
        return _dot(hb, w_ref[:, a:b]) + b_ref[:, a:b]

    u = jax.nn.gelu(sec(C_ZU, C_ZU + SGU_W))
    vv = _rms(jax.nn.gelu(sec(C_ZV, C_ZV + SGU_W)), gsgu_ref[...])
    if emit_v:
        rest[0][...] = vv
    vb = vv.astype(BF16)
    for g in range(SGU_G):
        cs = slice(g * SGU_C, (g + 1) * SGU_C)
        for c in range(tm // mix_rows):
            rs = slice(c * mix_rows, (c + 1) * mix_rows)
            z = _dot(wmix_ref[g], vb[rs, cs]) + bmix_ref[:, cs]
            oa_ref[rs, cs] = (u[rs, cs] * z).astype(oa_ref.dtype)

    q_ref[...] = sec(C_ZQ, C_ZQ + 512).astype(q_ref.dtype)
    k_ref[...] = (sec(C_ZK, C_ZK + 512) * (LDK ** -0.5)).astype(k_ref.dtype)
    v_ref[...] = sec(C_ZVV, C_ZVV + 512).astype(v_ref.dtype)
    og_ref[...] = _sigmoid(sec(C_ZO, C_ZO + 512)).astype(og_ref.dtype)
    zc_ref[...] = sec(C_ZC, C_ZC + 512)
    small_ref[...] = sec(C_SMALL, C_SMALL + LANE)
    for n in range(3):
        a = C_GATE + n * D
        gates_ref[:, n * D:(n + 1) * D] = _sigmoid(sec(a, a + D)).astype(gates_ref.dtype)


def _proj(x2, sc, sh, wcat, bcat, g1, gsgu, wmix, bmix, *, tm, rows_per_mod, emit_v):
    m = x2.shape[0]
    mix_rows = wmix.shape[-1]
    if rows_per_mod is None:
        mod_spec = pl.BlockSpec((tm, D), lambda i: (i, 0))
    else:
        tpb = rows_per_mod // tm
        mod_spec = pl.BlockSpec((None, 1, D), lambda i: (i // tpb, 0, 0))
    row = lambda w: pl.BlockSpec((tm, w), lambda i: (i, 0))
    out_shape = [
        jax.ShapeDtypeStruct((m, 512), BF16),
        jax.ShapeDtypeStruct((m, 512), BF16),
        jax.ShapeDtypeStruct((m, 512), BF16),
        jax.ShapeDtypeStruct((m, 512), BF16),
        jax.ShapeDtypeStruct((m, 512), BF16),
        jax.ShapeDtypeStruct((m, 512), F32),
        jax.ShapeDtypeStruct((m, LANE), F32),
        jax.ShapeDtypeStruct((m, 3 * D), BF16),
    ]
    out_specs = [row(512)] * 5 + [row(512), row(LANE), row(3 * D)]
    if emit_v:
        out_shape.append(jax.ShapeDtypeStruct((m, SGU_W), F32))
        out_specs.append(row(SGU_W))
    return pl.pallas_call(
        functools.partial(_proj_kernel, mix_rows=mix_rows, emit_v=emit_v),
        out_shape=out_shape,
        grid=(m // tm,),
        in_specs=[
            row(D), mod_spec, mod_spec,
            _const_spec(wcat.shape), _const_spec(bcat.shape), _const_spec(g1.shape), _const_spec(gsgu.shape),
            _const_spec(wmix.shape), _const_spec(bmix.shape),
        ],
        out_specs=out_specs,
        compiler_params=_cparams(("arbitrary",)),
        name="proj",
    )(x2, sc, sh, wcat, bcat, g1, gsgu, wmix, bmix)


def _mla_kernel(zc_ref, small_ref, cos_ref, sin_ref, wq_ref, wqr_ref, gq_ref, gqr_ref, wuk_ref, wuv_ref, vone_ref,
                gcq_ref, gckv_ref, gkn_ref, gkp_ref, wukt_ref,
                ckv_ref, kpe_ref, ksc_ref, *outs, sample):
    tm = zc_ref.shape[0]
    cosv = cos_ref[...]
    sinv = sin_ref[...]
    lane = lax.broadcasted_iota(jnp.int32, (tm, LANE), 1)
    scale = A_QK ** -0.5

    cq = _rms(zc_ref[:, 0:A_QR], gcq_ref[...]).astype(BF16)
    q = _dot(cq, wq_ref[...])
    qr = _dot(cq, wqr_ref[...])

    ckv = _rms(zc_ref[:, A_QR:A_QR + A_KVR], gckv_ref[...])
    ckv_ref[...] = ckv
    ckvb = ckv.astype(BF16)
    kn = _dot(ckvb, wuk_ref[...])

    small = small_ref[...]
    zkr = jnp.where((lane >= SM_KR) & (lane < SM_KR + A_ROPE), small, 0.0)
    t = zkr * gkp_ref[...]
    half = A_ROPE // 2
    rot = jnp.where(lane < SM_KR + half, pltpu.roll(t, LANE - half, 1), pltpu.roll(t, half, 1))
    kpe = t * cosv + rot * sinv
    kpe_ref[...] = kpe[:, SM_KR:SM_KR + A_ROPE]

    ksc = jnp.zeros((tm, LANE), F32)
    if sample:
        qa_ref, ql_ref = outs
    else:
        qa_ref, ka_ref, vt_ref = outs
        vt_ref[...] = (_dot_nt(wuv_ref[...], ckvb) + vone_ref[...]).astype(vt_ref.dtype)
    for h in range(AH):
        hs = slice(h * LANE, (h + 1) * LANE)
        qb = q[:, hs]
        rinv = lax.rsqrt(jnp.sum(qb * qb, axis=-1, keepdims=True) / A_QK + EPS)
        qh = (qb * rinv * gq_ref[:, hs] * cosv + qr[:, hs] * rinv * gqr_ref[:, hs] * sinv)
        kraw = kn[:, hs] + zkr
        ks = lax.rsqrt(jnp.sum(kraw * kraw, axis=-1, keepdims=True) / A_QK + EPS)
        ksc = jnp.where(lane == h, ks, ksc)
        if sample:
            qa_ref[:, hs] = qh * scale
            qn = (qh * gkn_ref[...]).astype(BF16)
            ql_ref[:, h * A_KVR:(h + 1) * A_KVR] = _dot(qn, wukt_ref[h]) * scale
        else:
            qa_ref[:, hs] = (qh * scale).astype(qa_ref.dtype)
            ka_ref[:, hs] = ((kn[:, hs] * gkn_ref[...] + kpe) * ks).astype(ka_ref.dtype)
    ksc_ref[...] = ksc[:, 0:AH]


def _mla_prep(zc, small, cos_t, sin_t, lw, *, tm, rows_per_pos, sample):
    m = zc.shape[0]
    if rows_per_pos is None:
        pos_spec = pl.BlockSpec((tm, LANE), lambda i: (i, 0))
    else:
        tpb = rows_per_pos // tm
        pos_spec = pl.BlockSpec((tm, LANE), lambda i: (i % tpb, 0))
    row = lambda w: pl.BlockSpec((tm, w), lambda i: (i, 0))
    out_shape = [
        jax.ShapeDtypeStruct((m, A_KVR), F32),
        jax.ShapeDtypeStruct((m, A_ROPE), F32),
        jax.ShapeDtypeStruct((m, AH), F32),
    ]
    out_specs = [row(A_KVR), row(A_ROPE), row(AH)]
    if sample:
        out_shape += [jax.ShapeDtypeStruct((m, AH * LANE), F32), jax.ShapeDtypeStruct((m, AH * A_KVR), F32)]
        out_specs += [row(AH * LANE), row(AH * A_KVR)]
    else:
        n_seq = m // rows_per_pos
        out_shape += [jax.ShapeDtypeStruct((m, AH * LANE), BF16)] * 2
        out_shape += [jax.ShapeDtypeStruct((n_seq * AH * LANE, rows_per_pos), BF16)]
        out_specs += [row(AH * LANE)] * 2
        out_specs += [pl.BlockSpec((AH * LANE, tm), lambda i: (i // tpb, i % tpb))]
    names =("wq", "wqr", "gq", "gqr", "wuk", "wuv", "vone", "gcq", "gckv", "gkn", "gkp", "wukt")
    ws = [lw[n] for n in names]
    return pl.pallas_call(
        functools.partial(_mla_kernel, sample=sample),
        out_shape=out_shape,
        grid=(m // tm,),
        in_specs=[row(512), row(LANE), pos_spec, pos_spec] + [_const_spec(w.shape) for w in ws],
        out_specs=out_specs,
        compiler_params=_cparams(("arbitrary",)),
        name="mla_prep",
    )(zc, small, cos_t, sin_t, *ws)


def _split3(x):
    hi = x.astype(BF16)
    r1 = x - hi.astype(F32)
    mid = r1.astype(BF16)
    lo = (r1 - mid.astype(F32)).astype(BF16)
    return hi, mid, lo


def _mlstm_kernel(q_ref, k_ref, v_ref, og_ref, small_ref, ghb_ref, c0_ref, n0_ref, m0_ref,
                  h_ref, c_ref, n_ref, m_ref, cnt_sc, *, n_chunks):
    n_seq, L = q_ref.shape[0], q_ref.shape[1]
    first = pl.program_id(1) == 0
    final = pl.program_id(1) == n_chunks - 1

    @pl.when(first)
    def _():
        m_ref[...] = m0_ref[...]
        for i in range(n_seq):
            for h in range(LH):
                cnt_sc[i, h, 0:LDV, :] = c0_ref[i, h].T
                cnt_sc[i, h, LDV:2 * LDV, :] = jnp.broadcast_to(n0_ref[i, h:h + 1, :], (LDV, LDK))

    s_idx = lax.broadcasted_iota(jnp.int32, (L, L), 0)
    t_idx = lax.broadcasted_iota(jnp.int32, (L, L), 1)
    upto = s_idx <= t_idx
    tril = jnp.where(t_idx <= s_idx, 1.0, 0.0).astype(BF16)
    lane = lax.broadcasted_iota(jnp.int32, (1, LANE), 1)
    ones = jnp.ones((L, LDV), BF16)
    for i in range(n_seq):
        _mlstm_chunk(i, q_ref, k_ref, v_ref, og_ref, small_ref, ghb_ref, h_ref, m_ref, cnt_sc,
                     upto, tril, lane, ones)

    @pl.when(final)
    def _():
        for i in range(n_seq):
            for h in range(LH):
                c_ref[i, h] = cnt_sc[i, h, 0:LDV, :].T
                n_ref[i, h:h + 1, :] = cnt_sc[i, h, LDV:LDV + 1, :]


def _mlstm_chunk(i, q_ref, k_ref, v_ref, og_ref, small_ref, ghb_ref, h_ref, m_ref, cnt_sc, upto, tril, lane, ones):
    L = q_ref.shape[1]
    small = small_ref[i]
    m_all = m_ref[i]
    m_out = m_all
    for h in range(LH):
        hs = slice(h * LDK, (h + 1) * LDK)
        zf = small[:, SM_F + h:SM_F + h + 1]
        lf = jnp.broadcast_to(jnp.minimum(zf, 0.0) - jnp.log1p(jnp.exp(-jnp.abs(zf))), (L, LDK))
        ig = jnp.broadcast_to(small[:, SM_I + h:SM_I + h + 1], (L, LDK))
        b_row = jnp.sum(jnp.where(upto, lf[:, 0:L], 0.0), axis=0, keepdims=True)
        b_bc = functools.reduce(jnp.add, [_dot(tril, piece) for piece in _split3(lf)])
        g = ig - b_bc
        logw = jnp.where(upto, b_row + g[:, 0:L], NEG_INF)
        m_prev = m_all[:, h:h + 1]
        m_inter = m_prev + b_row
        m_t = jnp.maximum(m_inter, jnp.max(logw, axis=0, keepdims=True))
        qh = q_ref[i, :, hs]
        kh = k_ref[i, :, hs]
        v1 = jnp.concatenate([v_ref[i, :, hs], ones], axis=1)
        smat = _dot_nt(kh, qh) * jnp.exp(logw - m_t)
        a_inter = jnp.exp(m_inter - m_t)
        cnt = cnt_sc[i, h]
        nd = (lax.dot_general(v1, smat.astype(BF16), (((0,), (0,)), ((), ())), preferred_element_type=F32)
              + a_inter * _dot_nt(cnt.astype(BF16), qh))
        hv = nd[0:LDV] / jnp.maximum(jnp.abs(nd[LDV:2 * LDV]), jnp.exp(-m_t))
        hn = hv * lax.rsqrt(jnp.mean(hv * hv, axis=0, keepdims=True) + EPS) * ghb_ref[h, :, 0:L]
        h_ref[i, :, hs] = (og_ref[i, :, hs].astype(F32) * hn.T).astype(h_ref.dtype)
        m_new = m_t[:, L - 1:L]
        b_last = b_row[:, L - 1:L]
        decay = jnp.exp(m_prev + b_last - m_new)
        wk = (kh.astype(F32) * jnp.exp(g + (b_last - m_new))).astype(BF16)
        cnt_new = decay * cnt + lax.dot_general(v1, wk, (((0,), (0,)), ((), ())), preferred_element_type=F32)
        cnt_sc[i, h] = cnt_new
        m_out = jnp.where(lane == h, m_new, m_out)
    m_ref[i] = m_out


MLSTM_SEQS = 4


def _mlstm(q, k, v, og, small, gh, c0, n0, m0, *, n_chunks, layer):
    m = q.shape[0]
    b = c0.shape[1]
    L = m // (b * n_chunks)
    assert L <= LCHUNK and L % BF16_ROWS == 0
    ns = MLSTM_SEQS
    seq3 = lambda a: a.reshape(b, n_chunks * L, a.shape[-1])
    row = lambda w: pl.BlockSpec((ns, L, w), lambda i, c: (i, c, 0))
    st_c = pl.BlockSpec((ns, LH, LDK, LDV), lambda i, c: (i, 0, 0, 0))
    st_n = pl.BlockSpec((ns, LH, LDK), lambda i, c: (i, 0, 0))
    st_m = pl.BlockSpec((ns, 1, LANE), lambda i, c: (i, 0, 0))
    in_c = pl.BlockSpec((None, ns, LH, LDK, LDV), lambda i, c: (layer, i, 0, 0, 0))
    in_n = pl.BlockSpec((None, ns, LH, LDK), lambda i, c: (layer, i, 0, 0))
    h3, c_f, n_f, m_f = pl.pallas_call(
        functools.partial(_mlstm_kernel, n_chunks=n_chunks),
        out_shape=[
            jax.ShapeDtypeStruct((b, n_chunks * L, 512), BF16),
            jax.ShapeDtypeStruct((b, LH, LDK, LDV), F32),
            jax.ShapeDtypeStruct((b, LH, LDK), F32),
            jax.ShapeDtypeStruct((b, 1, LANE), F32),
        ],
        grid=(b // ns, n_chunks),
        in_specs=[row(512), row(512), row(512), row(512), row(LANE), _const_spec(gh.shape), in_c, in_n, st_m],
        out_specs=[row(512), st_c, st_n, st_m],
        scratch_shapes=[pltpu.VMEM((ns, LH, 2 * LDV, LDK), F32)],
        compiler_params=_cparams(("arbitrary", "arbitrary")),
        name="mlstm",
    )(seq3(q), seq3(k), seq3(v), seq3(og), seq3(small), gh, c0, n0, m0)
    return h3.reshape(m, 512), c_f, n_f, m_f


def _attn_kernel(q_ref, k_ref, vt_ref, o_ref, p_sc, *, tq):
    t = q_ref.shape[0]
    key = lax.broadcasted_iota(jnp.int32, (tq, tq), 0)
    qry = lax.broadcasted_iota(jnp.int32, (tq, tq), 1)
    for i in range(t // tq):
        r0, r1 = i * tq, (i + 1) * tq
        st = _dot_nt(k_ref[0:r1, :], q_ref[r0:r1, :])
        st_d = jnp.where(key <= qry, st[r0:], NEG_INF)
        m = jnp.max(st_d, axis=0, keepdims=True)
        if i > 0:
            m = jnp.maximum(m, jnp.max(st[:r0], axis=0, keepdims=True))
            p_sc[0:r0] = jnp.exp(st[:r0] - m).astype(BF16)
        p_sc[r0:r1] = jnp.exp(st_d - m).astype(BF16)
        ot = _dot(vt_ref[:, 0:r1], p_sc[0:r1])
        o_ref[r0:r1, :] = (ot / ot[A_V:A_V + 1, :]).T.astype(o_ref.dtype)


def _attn_prompt(qa, ka, vt, *, b, t, tq):
    blk = pl.BlockSpec((t, LANE), lambda i, h: (i, h))
    return pl.pallas_call(
        functools.partial(_attn_kernel, tq=tq),
        out_shape=jax.ShapeDtypeStruct(qa.shape, BF16),
        grid=(b, AH),
        in_specs=[blk, blk, pl.BlockSpec((LANE, t), lambda i, h: (i * AH + h, 0))],
        out_specs=blk,
        scratch_shapes=[pltpu.VMEM((t, tq), BF16)],
        compiler_params=_cparams(("arbitrary", "arbitrary")),
        name="attn_prompt",
    )(qa, ka, vt)


def _paged_kernel(pt_ref, ql_ref, qp_ref, cn_ref, kpn_ref, ksn_ref, ckv_hbm, kpe_hbm, ks_hbm,
                  o_ref, cbuf, kpbuf, ksbuf, sems, m_sc, l_sc, acc_sc, *, layer, n_per_step, n_chains, t_new):
    j = pl.program_id(1)
    steps = pl.num_programs(1)
    step = pl.program_id(0) * steps + j
    last = pl.num_programs(0) * steps - 1
    slot = step % PAGED_SLOTS
    ql = ql_ref[...]
    qp = qp_ref[...]
    nr = ql.shape[0]

    def page_copies(step_, n):
        slot_ = step_ % PAGED_SLOTS
        pg = pt_ref[jnp.minimum(step_, last) * n_per_step + n]
        return (pltpu.make_async_copy(ckv_hbm.at[layer, pg], cbuf.at[slot_, n], sems.at[slot_, 0]),
                pltpu.make_async_copy(kpe_hbm.at[layer, pg], kpbuf.at[slot_, n], sems.at[slot_, 1]),
                pltpu.make_async_copy(ks_hbm.at[layer, pg], ksbuf.at[slot_, n], sems.at[slot_, 2]))

    @pl.when(step == 0)
    def _():
        for ahead in range(PAGED_SLOTS - 1):
            for n in range(n_per_step):
                for cp in page_copies(ahead, n):
                    cp.start()

    for n in range(n_per_step):
        for cp in page_copies(step, n):
            cp.wait()

    @pl.when(j == 0)
    def _():
        cn = cn_ref[...].astype(BF16)
        s = (_dot_nt(ql, cn) + _dot(qp, kpn_ref[...].astype(BF16))) * ksn_ref[...]
        nk = cn.shape[0]
        r_t = lax.broadcasted_iota(jnp.int32, (nr, nk), 0) // AH
        key = lax.broadcasted_iota(jnp.int32, (nr, nk), 1)
        s = jnp.where((key <= r_t) & (key < t_new), s, NEG_INF)
        m0 = jnp.max(s, axis=-1, keepdims=True)
        p = jnp.exp(s - m0)
        m_sc[0] = m0
        l_sc[0] = jnp.sum(p, axis=-1, keepdims=True)
        acc_sc[0] = _dot(p.astype(BF16), cn)
        for c in range(1, n_chains):
            m_sc[c] = m0
            l_sc[c] = jnp.zeros_like(m0)
            acc_sc[c] = jnp.zeros((nr, A_KVR), F32)

    per_chain = n_per_step // n_chains
    cbs, ss = [], []
    for n in range(n_per_step):
        for cp in page_copies(step + PAGED_SLOTS - 1, n):
            cp.start()
        cb = cbuf[slot, n].astype(BF16)
        kpt = kpbuf[slot, n].astype(BF16)
        ks = ksbuf[slot, n]
        kst = jnp.concatenate([ks] * t_new, axis=0)
        cbs.append(cb)
        ss.append((_dot_nt(ql, cb) + _dot(qp, kpt)) * kst)
    finals = []
    for c in range(n_chains):
        sc = ss[c * per_chain:(c + 1) * per_chain]
        cc = cbs[c * per_chain:(c + 1) * per_chain]
        m_i = m_sc[c]
        m_new = jnp.maximum(m_i, jnp.max(functools.reduce(jnp.maximum, sc), axis=-1, keepdims=True))
        alpha = jnp.exp(m_i - m_new)
        ps = [jnp.exp(s - m_new) for s in sc]
        pv = functools.reduce(jnp.add, [_dot(p.astype(BF16), cb) for p, cb in zip(ps, cc)])
        l_i = alpha * l_sc[c] + jnp.sum(functools.reduce(jnp.add, ps), axis=-1, keepdims=True)
        acc = alpha * acc_sc[c] + pv
        m_sc[c] = m_new
        l_sc[c] = l_i
        acc_sc[c] = acc
        finals.append((m_new, l_i, acc))

    @pl.when(j == steps - 1)
    def _():
        m_f = functools.reduce(jnp.maximum, [f[0] for f in finals])
        ws = [jnp.exp(f[0] - m_f) for f in finals]
        l_f = functools.reduce(jnp.add, [w * f[1] for w, f in zip(ws, finals)])
        acc_f = functools.reduce(jnp.add, [w * f[2] for w, f in zip(ws, finals)])
        o_ref[...] = acc_f / l_f

    @pl.when(step == last)
    def _():
        for ahead in range(1, PAGED_SLOTS):
            for n in range(n_per_step):
                for cp in page_copies(last + ahead, n):
                    cp.wait()


def _attn_paged(page_table, qlat, qp, c_new, kp_new_t, ks_new_t, cache_ckv, cache_kpe_t, cache_ks_t, *, layer, t_new,
                n_per_step, n_chains):
    bs, nr, _ = qlat.shape
    n_pages = page_table.shape[1]
    nk = c_new.shape[1]
    steps = n_pages // n_per_step
    pt_flat = page_table.reshape(-1)

    per_b = lambda s1, s2: pl.BlockSpec((None, s1, s2), lambda b, j, pt: (b, 0, 0))
    hbm = pl.BlockSpec(memory_space=pl.ANY)
    grid_spec = pltpu.PrefetchScalarGridSpec(
        num_scalar_prefetch=1,
        grid=(bs, steps),
        in_specs=[per_b(nr, A_KVR), per_b(nr, A_ROPE), per_b(nk, A_KVR), per_b(A_ROPE, nk), per_b(nr, nk),
                  hbm, hbm, hbm],
        out_specs=per_b(nr, A_KVR),
        scratch_shapes=[
            pltpu.VMEM((PAGED_SLOTS, n_per_step, PAGE, A_KVR), F32),
            pltpu.VMEM((PAGED_SLOTS, n_per_step, A_ROPE, PAGE), F32),
            pltpu.VMEM((PAGED_SLOTS, n_per_step, AH, PAGE), F32),
            pltpu.SemaphoreType.DMA((PAGED_SLOTS, 3)),
            pltpu.VMEM((n_chains, nr, 1), F32), pltpu.VMEM((n_chains, nr, 1), F32),
            pltpu.VMEM((n_chains, nr, A_KVR), F32),
        ],
    )
    return pl.pallas_call(
        functools.partial(_paged_kernel, layer=layer, n_per_step=n_per_step, n_chains=n_chains, t_new=t_new),
        out_shape=jax.ShapeDtypeStruct((bs, nr, A_KVR), F32),
        grid_spec=grid_spec,
        compiler_params=_cparams(("arbitrary", "arbitrary")),
        name="attn_paged",
    )(pt_flat, qlat, qp, c_new, kp_new_t, ks_new_t, cache_ckv, cache_kpe_t, cache_ks_t)


def _uv_kernel(x_ref, w_ref, o_ref):
    o_ref[...] = _dot(x_ref[...].astype(BF16), w_ref[...]).astype(o_ref.dtype)


def _uv_proj(olat_h, wuvp):
    _, m, _ = olat_h.shape
    return pl.pallas_call(
        _uv_kernel,
        out_shape=jax.ShapeDtypeStruct((m, AH * LANE), BF16),
        grid=(AH,),
        in_specs=[pl.BlockSpec((None, m, A_KVR), lambda h: (h, 0, 0)),
                  pl.BlockSpec((None, A_KVR, LANE), lambda h: (h, 0, 0))],
        out_specs=pl.BlockSpec((m, LANE), lambda h: (0, h)),
        compiler_params=_cparams(("arbitrary",)),
        name="uv_proj",
    )(olat_h, wuvp)


def _merge_kernel(x_ref, g1_ref, oa_ref, ob_ref, oc_ref, gates_ref, wa_ref, wb_ref, wc_ref, wo_ref, o_ref):
    mix = gates_ref[:, 0:D].astype(F32) * _dot(oa_ref[...], wa_ref[...])
    mix += gates_ref[:, D:2 * D].astype(F32) * _dot(ob_ref[...], wb_ref[...])
    mix += gates_ref[:, 2 * D:3 * D].astype(F32) * _dot(oc_ref[...], wc_ref[...])
    o_ref[...] = x_ref[...] + g1_ref[...] * _dot(mix.astype(BF16), wo_ref[...])


def _mod_spec(tm, rows_per_mod):
    if rows_per_mod is None:
        return pl.BlockSpec((tm, D), lambda i: (i, 0))
    tpb = rows_per_mod // tm
    return pl.BlockSpec((None, 1, D), lambda i: (i // tpb, 0, 0))


def _merge(x2, g1, oa, ob, oc, gates, wa, wb, wc, wo, *, tm, rows_per_mod):
    m = x2.shape[0]
    row = lambda w: pl.BlockSpec((tm, w), lambda i: (i, 0))
    return pl.pallas_call(
        _merge_kernel,
        out_shape=jax.ShapeDtypeStruct((m, D), F32),
        grid=(m // tm,),
        in_specs=[row(D), _mod_spec(tm, rows_per_mod), row(512), row(512), row(AH * LANE), row(3 * D),
                  _const_spec(wa.shape), _const_spec(wb.shape), _const_spec(wc.shape), _const_spec(wo.shape)],
        out_specs=row(D),
        compiler_params=_cparams(("arbitrary",)),
        name="merge",
    )(x2, g1, oa, ob, oc, gates, wa, wb, wc, wo)


def _moe_kernel(x_ref, sc_ref, sh_ref, g2_ref, gn_ref, wr1_ref, wr2_ref, br_ref, wg_ref, wu_ref, wd_ref, o_ref):
    tm = x_ref.shape[0]
    x = x_ref[...]
    h2 = _rms(x, gn_ref[...]) * (1.0 + sc_ref[...]) + sh_ref[...]
    hi = h2.astype(BF16)
    lo = (h2 - hi.astype(F32)).astype(BF16)
    r = _dot(hi, wr1_ref[...]) + _dot(lo, wr2_ref[...])
    rt = r.T
    logits = rt[0:32] + rt[32:64]
    scores = _sigmoid(logits)
    sel = scores + br_ref[...]
    e = [sel[8 * p:8 * p + 8] for p in range(GRP_E)]
    sc = [scores[8 * p:8 * p + 8] for p in range(GRP_E)]
    a, b = jnp.maximum(e[0], e[1]), jnp.minimum(e[0], e[1])
    c, d = jnp.maximum(e[2], e[3]), jnp.minimum(e[2], e[3])
    top1 = jnp.maximum(a, c)
    top2 = jnp.maximum(jnp.minimum(a, c), jnp.maximum(b, d))
    rowi = lax.broadcasted_iota(jnp.int32, (8, tm), 0).astype(F32)
    gs = jnp.where(rowi < N_GRP, top1 + top2, NEG_INF)
    gmax = jnp.max(gs, axis=0, keepdims=True)
    gidx = jnp.min(jnp.where(gs == gmax, rowi, 8.0), axis=0, keepdims=True)
    oh = rowi == gidx
    s = [jnp.sum(jnp.where(oh, e[p], 0.0), axis=0, keepdims=True) for p in range(GRP_E)]
    w = [jnp.sum(jnp.where(oh, sc[p], 0.0), axis=0, keepdims=True) for p in range(GRP_E)]

    def first_argmax(vals):
        best = jnp.maximum(jnp.maximum(vals[0], vals[1]), jnp.maximum(vals[2], vals[3]))
        return jnp.where(vals[0] == best, 0, jnp.where(vals[1] == best, 1, jnp.where(vals[2] == best, 2, 3)))

    i1 = first_argmax(s)
    i2 = first_argmax([jnp.where(i1 == p, NEG_INF, s[p]) for p in range(GRP_E)])
    w1 = sum(jnp.where(i1 == p, w[p], 0.0) for p in range(GRP_E))
    w2 = sum(jnp.where(i2 == p, w[p], 0.0) for p in range(GRP_E))
    den = w1 + w2
    w1 = w1 / den
    w2 = w2 / den
    comb = [jnp.where(oh, jnp.where(i1 == p, w1, 0.0) + jnp.where(i2 == p, w2, 0.0), 0.0) for p in range(GRP_E)]
    comb_t = jnp.concatenate(comb + [jnp.zeros((LANE - 8 * GRP_E, tm), F32)], axis=0)
    combine = comb_t.T

    acc = jnp.zeros((tm, D), F32)
    for g in range(N_GRP):
        acts = []
        for p in range(GRP_E):
            ex = g * GRP_E + p
            hg = _dot(hi, wg_ref[ex])
            hu = _dot(hi, wu_ref[ex])
            cw = combine[:, 8 * p + g:8 * p + g + 1]
            acts.append((hg * _sigmoid(hg) * hu * cw).astype(BF16))
        acc += _dot(jnp.concatenate(acts, axis=1), wd_ref[g])
    o_ref[...] = x + g2_ref[...] * acc


def _moe(x2, sc, sh, g2, gn, wr1, wr2, br, wg, wu, wd, *, tm, rows_per_mod):
    m = x2.shape[0]
    row = lambda w: pl.BlockSpec((tm, w), lambda i: (i, 0))
    ms = _mod_spec(tm, rows_per_mod)
    return pl.pallas_call(
        _moe_kernel,
        out_shape=jax.ShapeDtypeStruct((m, D), F32),
        grid=(m // tm,),
        in_specs=[row(D), ms, ms, ms, _const_spec(gn.shape), _const_spec(wr1.shape), _const_spec(wr2.shape),
                  _const_spec(br.shape), _const_spec(wg.shape), _const_spec(wu.shape), _const_spec(wd.shape)],
        out_specs=row(D),
        compiler_params=_cparams(("arbitrary",)),
        name="moe",
    )(x2, sc, sh, g2, gn, wr1, wr2, br, wg, wu, wd)


def _head_block(nope, r1, r2):
    pad = jnp.zeros(nope.shape[:-1] + (LANE - A_QK,), nope.dtype)
    blk = jnp.concatenate([nope, r1, r2, pad], axis=-1)
    return blk.reshape(blk.shape[:-2] + (AH * LANE,))


def _prep_layer(l, w_in, b_in, w_gate, b_gate, g_norm1, g_norm2, g_sgu, g_mlstm, g_cq, w_q_up, g_ckv, w_uk, w_uv,
                g_qn, g_kn, w_br, w_o, w_exp_gate, w_exp_up, w_exp_down):
    half = A_ROPE // 2
    wi, bi = w_in[l], b_in[l]

    def cat_cols(a, gate):
        z = lambda n: jnp.zeros(a.shape[:-1] + (n,), a.dtype)
        small = jnp.concatenate([z(SM_KR), a[..., 3592:3624], a[..., 3072:3080], z(LANE - SM_F - LH)], axis=-1)
        return jnp.concatenate([a[..., :3072], a[..., 3080:3592], small, gate], axis=-1)

    lw = {}
    lw["wcat"] = cat_cols(wi, w_gate[l]).astype(BF16)
    lw["bcat"] = cat_cols(bi, b_gate[l]).reshape(1, N_CAT)
    lw["g1"] = g_norm1[l].reshape(1, D)
    lw["g2"] = g_norm2[l].reshape(1, D)
    lw["gsgu"] = g_sgu[l].reshape(1, SGU_W)
    lw["gh"] = jnp.broadcast_to(g_mlstm[l][:, :, None], (LH, LDV, LCHUNK))
    wq = w_q_up[l]
    nope, r1, r2 = wq[..., :A_NOPE], wq[..., A_NOPE:A_NOPE + half], wq[..., A_NOPE + half:]
    lw["wq"] = _head_block(nope, r1, r2).astype(BF16)
    lw["wqr"] = _head_block(jnp.zeros_like(nope), r2, r1).astype(BF16)
    gq = jnp.broadcast_to(g_qn[l], (AH, A_QK))
    gn_, g1_, g2_ = gq[:, :A_NOPE], gq[:, A_NOPE:A_NOPE + half], gq[:, A_NOPE + half:]
    lw["gq"] = _head_block(gn_, g1_, g2_).reshape(1, AH * LANE)
    lw["gqr"] = _head_block(jnp.zeros_like(gn_), g2_, g1_).reshape(1, AH * LANE)
    zr = jnp.zeros((A_KVR, AH, half), F32)
    lw["wuk"] = _head_block(w_uk[l], zr, zr).astype(BF16)
    uv = jnp.concatenate([w_uv[l], jnp.zeros((A_KVR, AH, LANE - A_V), F32)], axis=-1)
    lw["wuv"] = uv.reshape(A_KVR, AH * LANE).T.astype(BF16)
    lw["wuvp"] = jnp.transpose(uv, (1, 0, 2)).astype(BF16)
    lw["vone"] = jnp.tile((jnp.arange(LANE) == A_V).astype(F32), AH).reshape(AH * LANE, 1)
    lw["gcq"] = g_cq[l].reshape(1, A_QR)
    lw["gckv"] = g_ckv[l].reshape(1, A_KVR)
    gk = g_kn[l]
    lw["gkn"] = jnp.concatenate([gk[:A_NOPE], jnp.zeros((LANE - A_NOPE,), F32)]).reshape(1, LANE)
    lw["gkp"] = jnp.concatenate([jnp.zeros((SM_KR,), F32), gk[A_NOPE:], jnp.zeros((LANE - A_QK,), F32)]).reshape(1, LANE)
    ukt = jnp.transpose(w_uk[l], (1, 2, 0))
    lw["wukt"] = jnp.concatenate([ukt, jnp.zeros((AH, LANE - A_NOPE, A_KVR), F32)], axis=1).astype(BF16)
    lw["wa"] = w_br[l, 0].astype(BF16)
    lw["wb"] = w_br[l, 1].astype(BF16)
    wc = w_br[l, 2].reshape(AH, A_V, D)
    lw["wc"] = jnp.concatenate([wc, jnp.zeros((AH, LANE - A_V, D), F32)], axis=1).reshape(AH * LANE, D).astype(BF16)
    lw["wo"] = w_o[l].astype(BF16)
    lw["wg"] = w_exp_gate[l].astype(BF16)
    lw["wu"] = w_exp_up[l].astype(BF16)
    lw["wd"] = w_exp_down[l].reshape(N_GRP, GRP_E * D_EXP, D).astype(BF16)
    return lw


def _prep_router(w_router, b_router):
    wr = w_router.reshape(D, N_GRP, GRP_E).transpose(0, 2, 1)
    wr = jnp.concatenate([wr, jnp.zeros((D, GRP_E, 8 - N_GRP), F32)], axis=-1).reshape(D, 8 * GRP_E)
    hi = wr.astype(BF16)
    lo = (wr - hi.astype(F32)).astype(BF16)
    z = lambda n: jnp.zeros((D, n), BF16)
    wr1 = jnp.concatenate([hi, lo, z(LANE - 64)], axis=1)
    wr2 = jnp.concatenate([hi, z(LANE - 32)], axis=1)
    br = b_router.reshape(N_GRP, GRP_E).T
    br = jnp.concatenate([br, jnp.zeros((GRP_E, 8 - N_GRP), F32)], axis=-1).reshape(8 * GRP_E, 1)
    return wr1, wr2, br


def _rope_tables(pos):
    half = A_ROPE // 2
    freqs = ROPE_BASE ** (-jnp.arange(half, dtype=F32) / half)
    ang = pos[:, None] * freqs[None, :]
    cos, sin = jnp.cos(ang), jnp.sin(ang)
    n = pos.shape[0]
    cos_t = jnp.concatenate([jnp.ones((n, A_NOPE), F32), cos, cos, jnp.zeros((n, LANE - A_QK), F32)], axis=1)
    sin_t = jnp.concatenate([jnp.zeros((n, A_NOPE), F32), -sin, sin, jnp.zeros((n, LANE - A_QK), F32)], axis=1)
    return cos_t, sin_t


def _mix_weights(w_spatial_l, b_spatial_l, chunk, reps):
    w = jnp.where(jnp.tril(jnp.ones((chunk, chunk), dtype=bool))[None], w_spatial_l[:, :chunk, :chunk], 0.0)
    if reps > 1:
        idx = jnp.arange(reps * chunk)
        pos = jax.nn.one_hot(idx % chunk, chunk, dtype=F32)
        same = (idx[:, None] // chunk) == (idx[None, :] // chunk)
        w = jnp.where(same[None], jnp.einsum("rt,gts,cs->grc", pos, w, pos, precision=lax.Precision.HIGHEST), 0.0)
    bias = jnp.tile(b_spatial_l[:, :chunk].T, (reps, 1))
    bias = jnp.repeat(bias, SGU_C, axis=1)
    return w.astype(BF16), bias


def _layer_prompt(x2, mod, lw, router, tables, wmix, bmix, b, t):
    m = b * t
    tm = 512
    sh1, sc1, g1, sh2, sc2, g2 = [a.reshape(b, 1, D) for a in jnp.split(mod, 6, axis=-1)]
    oa, q, k, v, og, zc, small, gates = _proj(x2, sc1, sh1, lw["wcat"], lw["bcat"], lw["g1"], lw["gsgu"], wmix, bmix,
                                              tm=tm, rows_per_mod=t, emit_v=False)
    ckv, kpe, ksc, qa, ka, va = _mla_prep(zc, small, tables[0], tables[1], lw, tm=tm, rows_per_pos=t, sample=False)
    n_chunks = t // LCHUNK
    c0 = jnp.zeros((b, LH, LDK, LDV), F32)
    n0 = jnp.zeros((b, LH, LDK), F32)
    m0 = jnp.zeros((b, 1, LANE), F32)
    ob, c_f, n_f, m_f = _mlstm(q, k, v, og, small, lw["gh"], c0[None], n0[None], m0, n_chunks=n_chunks, layer=0)
    oc = _attn_prompt(qa, ka, va, b=b, t=t, tq=min(t, 256))
    x1 = _merge(x2, g1, oa, ob, oc, gates, lw["wa"], lw["wb"], lw["wc"], lw["wo"], tm=tm, rows_per_mod=t)
    x_out = _moe(x1, sc2, sh2, g2, lw["g2"], *router, lw["wg"], lw["wu"], lw["wd"], tm=tm, rows_per_mod=t)
    state = (ckv.reshape(b, t, A_KVR), kpe.reshape(b, t, A_ROPE), ksc.reshape(b, t, AH), c_f, n_f, m_f[:, 0, :LH])
    return x_out, state


def _layer_sample(x2, mod, lw, router, tables, wmix, bmix, b, t, st_c, st_n, st_m, caches, page_table, layer):
    m = b * t
    tm = m
    mods = [jnp.repeat(a, t, axis=0) for a in jnp.split(mod, 6, axis=-1)]
    sh1, sc1, g1, sh2, sc2, g2 = mods
    oa, q, k, v, og, zc, small, gates, v_sgu = _proj(x2, sc1, sh1, lw["wcat"], lw["bcat"], lw["g1"], lw["gsgu"], wmix,
                                                     bmix, tm=tm, rows_per_mod=None, emit_v=True)
    ckv, kpe, ksc, qa, qlat = _mla_prep(zc, small, tables[0], tables[1], lw, tm=tm, rows_per_pos=None, sample=True)

    lpad = -(-t // BF16_ROWS) * BF16_ROWS

    def pad_rows(a, fill=None):
        a3 = a.reshape(b, t, a.shape[-1])
        if fill is None:
            padv = jnp.zeros((b, lpad - t, a.shape[-1]), a.dtype)
        else:
            padv = jnp.broadcast_to(fill, (b, lpad - t, a.shape[-1])).astype(a.dtype)
        return jnp.concatenate([a3, padv], axis=1).reshape(b * lpad, a.shape[-1])

    lane = jnp.arange(LANE)
    inert = jnp.where((lane >= SM_I) & (lane < SM_I + LH), -1e30, jnp.where((lane >= SM_F) & (lane < SM_F + LH), 1e30, 0.0))
    m0 = jnp.concatenate([st_m, jnp.zeros((b, LANE - LH), F32)], axis=1).reshape(b, 1, LANE)
    ob_p, c_f, n_f, m_f = _mlstm(pad_rows(q), pad_rows(k), pad_rows(v), pad_rows(og), pad_rows(small, inert.astype(F32)),
                                 lw["gh"], st_c, st_n, m0, n_chunks=1, layer=layer)
    ob = ob_p.reshape(b, lpad, 512)[:, :t].reshape(m, 512)

    nr = t * AH
    nk = 8
    ql3 = qlat.reshape(b, nr, A_KVR).astype(BF16)
    qp3 = qa.reshape(b, t, AH, LANE)[..., A_NOPE:A_QK].reshape(b, nr, A_ROPE).astype(BF16)
    padk = lambda a: jnp.concatenate([a.reshape(b, t, -1), jnp.zeros((b, nk - t, a.shape[-1]), F32)], axis=1)
    ks_t = jnp.transpose(padk(ksc), (0, 2, 1))
    ks_t = jnp.tile(ks_t, (1, t, 1))
    kp_t = jnp.transpose(padk(kpe), (0, 2, 1))
    olat = _attn_paged(page_table, ql3, qp3, padk(ckv), kp_t, ks_t, *caches, layer=layer, t_new=t, n_per_step=32,
                       n_chains=4)
    olat_h = jnp.transpose(olat.reshape(b, t, AH, A_KVR), (2, 0, 1, 3)).reshape(AH, m, A_KVR)
    oc = _uv_proj(olat_h, lw["wuvp"])

    x1 = _merge(x2, g1, oa, ob, oc, gates, lw["wa"], lw["wb"], lw["wc"], lw["wo"], tm=tm, rows_per_mod=None)
    x_out = _moe(x1, sc2, sh2, g2, lw["g2"], *router, lw["wg"], lw["wu"], lw["wd"], tm=tm, rows_per_mod=None)
    state = (ckv.reshape(b, t, A_KVR), kpe.reshape(b, t, A_ROPE), ksc.reshape(b, t, AH), c_f, n_f, m_f[:, 0, :LH],
             v_sgu.reshape(b, t, SGU_W))
    return x_out, state


def kernel(x_prompt, x_sample, c_prompt, c_sample, cache_ckv, cache_kpe, cache_kscale, page_table, state_C, state_n, state_m, w_ada, b_ada, g_norm1, g_norm2, w_in, b_in, g_sgu, w_spatial, b_spatial, g_mlstm, g_cq, w_q_up, g_ckv, w_uk, w_uv, g_qn, g_kn, w_br, w_gate, b_gate, w_o, w_router, b_router, w_exp_gate, w_exp_up, w_exp_down):
    bp, tp, _ = x_prompt.shape
    bs, ts, _ = x_sample.shape
    depth = w_in.shape[0]
    past_len = page_table.shape[1] * PAGE

    mod_all = _ada(jnp.concatenate([c_prompt, c_sample], axis=0), w_ada, b_ada)
    router = _prep_router(w_router, b_router)
    tab_p = _rope_tables(jnp.arange(tp, dtype=F32))
    pos_s = jnp.tile(jnp.arange(ts, dtype=F32) + past_len, bs)
    tab_s = _rope_tables(pos_s)
    caches = (cache_ckv, jnp.swapaxes(cache_kpe, 2, 3), jnp.swapaxes(cache_kscale, 2, 3))

    xp = x_prompt.reshape(bp * tp, D)
    xs = x_sample.reshape(bs * ts, D)
    st_p, st_s = [], []
    for l in range(depth):
        lw = _prep_layer(l, w_in, b_in, w_gate, b_gate, g_norm1, g_norm2, g_sgu, g_mlstm, g_cq, w_q_up, g_ckv, w_uk,
                         w_uv, g_qn, g_kn, w_br, w_o, w_exp_gate, w_exp_up, w_exp_down)
        chunk_p = min(tp, SGU_C)
        wmix_p, bmix_p = _mix_weights(w_spatial[l], b_spatial[l], chunk_p, 1)
        wmix_s, bmix_s = _mix_weights(w_spatial[l], b_spatial[l], ts, bs)
        xp, sp = _layer_prompt(xp, mod_all[l, :bp], lw, router, tab_p, wmix_p, bmix_p, bp, tp)
        xs, ss = _layer_sample(xs, mod_all[l, bp:], lw, router, tab_s, wmix_s, bmix_s, bs, ts, state_C, state_n,
                               state_m[l], caches, page_table, l)
        st_p.append(sp)
        st_s.append(ss)
    stack = lambda sts, i: jnp.stack([s[i] for s in sts])
    return (xp.reshape(bp, tp, D), xs.reshape(bs, ts, D),
            stack(st_p, 0), stack(st_p, 1), stack(st_p, 2), stack(st_p, 3), stack(st_p, 4), stack(st_p, 5),
            stack(st_s, 0), stack(st_s, 1), stack(st_s, 2), stack(st_s, 3), stack(st_s, 4), stack(st_s, 5),
            stack(st_s, 6))
```

```python
import functools
import math

import jax
import jax.numpy as jnp
from jax import lax
from jax.experimental import pallas as pl
from jax.experimental.pallas import tpu as pltpu

F32 = jnp.float32
BF16 = jnp.bfloat16
NEG_INF = float("-inf")

EPS = 1e-6
D = 1024
SGU_G = 4
SGU_C = 128
SGU_W = SGU_G * SGU_C
LH = 4
LDK = 128
LDV = 128
LCHUNK = 128
AH = 8
A_QR = 256
A_KVR = 256
A_NOPE = 64
A_ROPE = 32
A_QK = A_NOPE + A_ROPE
A_V = 64
ROPE_BASE = 10000.0
PAGE = 128
N_EXP = 16
N_GRP = 4
GRP_E = 4
D_EXP = 256

LANE = 128
BF16_ROWS = 16
PAGED_SLOTS = 3
C_ZU, C_ZV, C_ZQ, C_ZK, C_ZVV, C_ZO = 0, 512, 1024, 1536, 2048, 2560
C_ZC = 3072
C_SMALL = 3584
C_GATE = 3712
N_CAT = C_GATE + 3 * D
SM_KR, SM_I, SM_F = 64, 96, 100

VMEM_LIMIT = 56 * 1024 * 1024


def _cparams(sem):
    return pltpu.CompilerParams(dimension_semantics=sem, vmem_limit_bytes=VMEM_LIMIT)


def _const_spec(shape):
    nd = len(shape)
    return pl.BlockSpec(shape, lambda *_: (0,) * nd, pipeline_mode=pl.Buffered(1))


def _sigmoid(x):
    return 0.5 * jnp.tanh(0.5 * x) + 0.5


def _dot(a, b):
    return jnp.dot(a, b, preferred_element_type=F32)


def _dot_nt(a, b):
    return lax.dot_general(a, b, (((1,), (1,)), ((), ())), preferred_element_type=F32)


def _rms(x, g):
    return x * lax.rsqrt(jnp.mean(x * x, axis=-1, keepdims=True) + EPS) * g


def _ada_kernel(c_ref, w_ref, b_ref, o_ref):
    c = c_ref[...]
    s = (c * _sigmoid(c)).astype(BF16)
    o_ref[...] = _dot(s, w_ref[...].astype(BF16)) + b_ref[...]


def _ada(c_all, w_ada, b_ada):
    depth, _, n = w_ada.shape
    bc = c_all.shape[0]
    tn = 1536
    return pl.pallas_call(
        _ada_kernel,
        out_shape=jax.ShapeDtypeStruct((depth, bc, n), F32),
        grid=(depth, n // tn),
        in_specs=[
            pl.BlockSpec((bc, D), lambda l, j: (0, 0)),
            pl.BlockSpec((None, D, tn), lambda l, j: (l, 0, j)),
            pl.BlockSpec((None, 1, tn), lambda l, j: (l, 0, j)),
        ],
        out_specs=pl.BlockSpec((None, bc, tn), lambda l, j: (l, 0, j)),
        compiler_params=_cparams(("arbitrary", "arbitrary")),
        name="ada",
    )(c_all, w_ada, b_ada.reshape(depth, 1, n))


def _proj_kernel(x_ref, sc_ref, sh_ref, w_ref, b_ref, g1_ref, gsgu_ref, wmix_ref, bmix_ref,
                 oa_ref, q_ref, k_ref, v_ref, og_ref, zc_ref, small_ref, gates_ref, *rest, mix_rows, emit_v):
    tm = x_ref.shape[0]
    x = x_ref[...]
    h = _rms(x, g1_ref[...]) * (1.0 + sc_ref[...]) + sh_ref[...]
    hb = h.astype(BF16)

    def sec(a, b):
        return _dot(hb, w_ref[:, a:b]) + b_ref[:, a:b]

    u = jax.nn.gelu(sec(C_ZU, C_ZU + SGU_W))
    vv = _rms(jax.nn.gelu(sec(C_ZV, C_ZV + SGU_W)), gsgu_ref[...])
    if emit_v:
        rest[0][...] = vv
    vb = vv.astype(BF16)
    for g in range(SGU_G):
        cs = slice(g * SGU_C, (g + 1) * SGU_C)
        for c in range(tm // mix_rows):
            rs = slice(c * mix_rows, (c + 1) * mix_rows)
            z = _dot(wmix_ref[g], vb[rs, cs]) + bmix_ref[:, cs]
            oa_ref[rs, cs] = (u[rs, cs] * z).astype(oa_ref.dtype)

    q_ref[...] = sec(C_ZQ, C_ZQ + 512).astype(q_ref.dtype)
    k_ref[...] = (sec(C_ZK, C_ZK + 512) * (LDK ** -0.5)).astype(k_ref.dtype)
    v_ref[...] = sec(C_ZVV, C_ZVV + 512).astype(v_ref.dtype)
    og_ref[...] = _sigmoid(sec(C_ZO, C_ZO + 512)).astype(og_ref.dtype)
    zc_ref[...] = sec(C_ZC, C_ZC + 512)
    small_ref[...] = sec(C_SMALL, C_SMALL + LANE)
    for n in range(3):
        a = C_GATE + n * D
        gates_ref[:, n * D:(n + 1) * D] = _sigmoid(sec(a, a + D)).astype(gates_ref.dtype)


def _proj(x2, sc, sh, wcat, bcat, g1, gsgu, wmix, bmix, *, tm, rows_per_mod, emit_v):
    m = x2.shape[0]
    mix_rows = wmix.shape[-1]
    if rows_per_mod is None:
        mod_spec = pl.BlockSpec((tm, D), lambda i: (i, 0))
    else:
        tpb = rows_per_mod // tm
        mod_spec = pl.BlockSpec((None, 1, D), lambda i: (i // tpb, 0, 0))
    row = lambda w: pl.BlockSpec((tm, w), lambda i: (i, 0))
    out_shape = [
        jax.ShapeDtypeStruct((m, 512), BF16),
        jax.ShapeDtypeStruct((m, 512), BF16),
        jax.ShapeDtypeStruct((m, 512), BF16),
        jax.ShapeDtypeStruct((m, 512), BF16),
        jax.ShapeDtypeStruct((m, 512), BF16),
        jax.ShapeDtypeStruct((m, 512), F32),
        jax.ShapeDtypeStruct((m, LANE), F32),
        jax.ShapeDtypeStruct((m, 3 * D), BF16),
    ]
    out_specs = [row(512)] * 5 + [row(512), row(LANE), row(3 * D)]
    if emit_v:
        out_shape.append(jax.ShapeDtypeStruct((m, SGU_W), F32))
        out_specs.append(row(SGU_W))
    return pl.pallas_call(
        functools.partial(_proj_kernel, mix_rows=mix_rows, emit_v=emit_v),
        out_shape=out_shape,
        grid=(m // tm,),
        in_specs=[
            row(D), mod_spec, mod_spec,
            _const_spec(wcat.shape), _const_spec(bcat.shape), _const_spec(g1.shape), _const_spec(gsgu.shape),
            _const_spec(wmix.shape), _const_spec(bmix.shape),
        ],
        out_specs=out_specs,
        compiler_params=_cparams(("arbitrary",)),
        name="proj",
    )(x2, sc, sh, wcat, bcat, g1, gsgu, wmix, bmix)


def _mla_kernel(zc_ref, small_ref, cos_ref, sin_ref, wq_ref, wqr_ref, gq_ref, gqr_ref, wuk_ref, wuv_ref, vone_ref,
                gcq_ref, gckv_ref, gkn_ref, gkp_ref, wukt_ref,
                ckv_ref, kpe_ref, ksc_ref, *outs, sample):
    tm = zc_ref.shape[0]
    cosv = cos_ref[...]
    sinv = sin_ref[...]
    lane = lax.broadcasted_iota(jnp.int32, (tm, LANE), 1)
    scale = A_QK ** -0.5

    cq = _rms(zc_ref[:, 0:A_QR], gcq_ref[...]).astype(BF16)
    q = _dot(cq, wq_ref[...])
    qr = _dot(cq, wqr_ref[...])

    ckv = _rms(zc_ref[:, A_QR:A_QR + A_KVR], gckv_ref[...])
    ckv_ref[...] = ckv
    ckvb = ckv.astype(BF16)
    kn = _dot(ckvb, wuk_ref[...])

    small = small_ref[...]
    zkr = jnp.where((lane >= SM_KR) & (lane < SM_KR + A_ROPE), small, 0.0)
    t = zkr * gkp_ref[...]
    half = A_ROPE // 2
    rot = jnp.where(lane < SM_KR + half, pltpu.roll(t, LANE - half, 1), pltpu.roll(t, half, 1))
    kpe = t * cosv + rot * sinv
    kpe_ref[...] = kpe[:, SM_KR:SM_KR + A_ROPE]

    ksc = jnp.zeros((tm, LANE), F32)
    if sample:
        qa_ref, ql_ref = outs
    else:
        qa_ref, ka_ref, vt_ref = outs
        vt_ref[...] = (_dot_nt(wuv_ref[...], ckvb) + vone_ref[...]).astype(vt_ref.dtype)
    for h in range(AH):
        hs = slice(h * LANE, (h + 1) * LANE)
        qb = q[:, hs]
        rinv = lax.rsqrt(jnp.sum(qb * qb, axis=-1, keepdims=True) / A_QK + EPS)
        qh = (qb * rinv * gq_ref[:, hs] * cosv + qr[:, hs] * rinv * gqr_ref[:, hs] * sinv)
        kraw = kn[:, hs] + zkr
        ks = lax.rsqrt(jnp.sum(kraw * kraw, axis=-1, keepdims=True) / A_QK + EPS)
        ksc = jnp.where(lane == h, ks, ksc)
        if sample:
            qa_ref[:, hs] = qh * scale
            qn = (qh * gkn_ref[...]).astype(BF16)
            ql_ref[:, h * A_KVR:(h + 1) * A_KVR] = _dot(qn, wukt_ref[h]) * scale
        else:
            qa_ref[:, hs] = (qh * scale).astype(qa_ref.dtype)
            ka_ref[:, hs] = ((kn[:, hs] * gkn_ref[...] + kpe) * ks).astype(ka_ref.dtype)
    ksc_ref[...] = ksc[:, 0:AH]


def _mla_prep(zc, small, cos_t, sin_t, lw, *, tm, rows_per_pos, sample):
    m = zc.shape[0]
    if rows_per_pos is None:
        pos_spec = pl.BlockSpec((tm, LANE), lambda i: (i, 0))
    else:
        tpb = rows_per_pos // tm
        pos_spec = pl.BlockSpec((tm, LANE), lambda i: (i % tpb, 0))
    row = lambda w: pl.BlockSpec((tm, w), lambda i: (i, 0))
    out_shape = [
        jax.ShapeDtypeStruct((m, A_KVR), F32),
        jax.ShapeDtypeStruct((m, A_ROPE), F32),
        jax.ShapeDtypeStruct((m, AH), F32),
    ]
    out_specs = [row(A_KVR), row(A_ROPE), row(AH)]
    if sample:
        out_shape += [jax.ShapeDtypeStruct((m, AH * LANE), F32), jax.ShapeDtypeStruct((m, AH * A_KVR), F32)]
        out_specs += [row(AH * LANE), row(AH * A_KVR)]
    else:
        n_seq = m // rows_per_pos
        out_shape += [jax.ShapeDtypeStruct((m, AH * LANE), BF16)] * 2
        out_shape += [jax.ShapeDtypeStruct((n_seq * AH * LANE, rows_per_pos), BF16)]
        out_specs += [row(AH * LANE)] * 2
        out_specs += [pl.BlockSpec((AH * LANE, tm), lambda i: (i // tpb, i % tpb))]
    names =("wq", "wqr", "gq", "gqr", "wuk", "wuv", "vone", "gcq", "gckv", "gkn", "gkp", "wukt")
    ws = [lw[n] for n in names]
    return pl.pallas_call(
        functools.partial(_mla_kernel, sample=sample),
        out_shape=out_shape,
        grid=(m // tm,),
        in_specs=[row(512), row(LANE), pos_spec, pos_spec] + [_const_spec(w.shape) for w in ws],
        out_specs=out_specs,
        compiler_params=_cparams(("arbitrary",)),
        name="mla_prep",
    )(zc, small, cos_t, sin_t, *ws)


def _split3(x):
    hi = x.astype(BF16)
    r1 = x - hi.astype(F32)
    mid = r1.astype(BF16)
    lo = (r1 - mid.astype(F32)).astype(BF16)
    return hi, mid, lo


def _mlstm_kernel(q_ref, k_ref, v_ref, og_ref, small_ref, ghb_ref, c0_ref, n0_ref, m0_ref,
                  h_ref, c_ref, n_ref, m_ref, cnt_sc, *, n_chunks):
    n_seq, L = q_ref.shape[0], q_ref.shape[1]
    first = pl.program_id(1) == 0
    final = pl.program_id(1) == n_chunks - 1

    @pl.when(first)
    def _():
        m_ref[...] = m0_ref[...]
        for i in range(n_seq):
            for h in range(LH):
                cnt_sc[i, h, 0:LDV, :] = c0_ref[i, h].T
                cnt_sc[i, h, LDV:2 * LDV, :] = jnp.broadcast_to(n0_ref[i, h:h + 1, :], (LDV, LDK))

    s_idx = lax.broadcasted_iota(jnp.int32, (L, L), 0)
    t_idx = lax.broadcasted_iota(jnp.int32, (L, L), 1)
    upto = s_idx <= t_idx
    tril = jnp.where(t_idx <= s_idx, 1.0, 0.0).astype(BF16)
    lane = lax.broadcasted_iota(jnp.int32, (1, LANE), 1)
    ones = jnp.ones((L, LDV), BF16)
    for i in range(n_seq):
        _mlstm_chunk(i, q_ref, k_ref, v_ref, og_ref, small_ref, ghb_ref, h_ref, m_ref, cnt_sc,
                     upto, tril, lane, ones)

    @pl.when(final)
    def _():
        for i in range(n_seq):
            for h in range(LH):
                c_ref[i, h] = cnt_sc[i, h, 0:LDV, :].T
                n_ref[i, h:h + 1, :] = cnt_sc[i, h, LDV:LDV + 1, :]


def _mlstm_chunk(i, q_ref, k_ref, v_ref, og_ref, small_ref, ghb_ref, h_ref, m_ref, cnt_sc, upto, tril, lane, ones):
    L = q_ref.shape[1]
    small = small_ref[i]
    m_all = m_ref[i]
    m_out = m_all
    for h in range(LH):
        hs = slice(h * LDK, (h + 1) * LDK)
        zf = small[:, SM_F + h:SM_F + h + 1]
        lf = jnp.broadcast_to(jnp.minimum(zf, 0.0) - jnp.log1p(jnp.exp(-jnp.abs(zf))), (L, LDK))
        ig = jnp.broadcast_to(small[:, SM_I + h:SM_I + h + 1], (L, LDK))
        b_row = jnp.sum(jnp.where(upto, lf[:, 0:L], 0.0), axis=0, keepdims=True)
        b_bc = functools.reduce(jnp.add, [_dot(tril, piece) for piece in _split3(lf)])
        g = ig - b_bc
        logw = jnp.where(upto, b_row + g[:, 0:L], NEG_INF)
        m_prev = m_all[:, h:h + 1]
        m_inter = m_prev + b_row
        m_t = jnp.maximum(m_inter, jnp.max(logw, axis=0, keepdims=True))
        qh = q_ref[i, :, hs]
        kh = k_ref[i, :, hs]
        v1 = jnp.concatenate([v_ref[i, :, hs], ones], axis=1)
        smat = _dot_nt(kh, qh) * jnp.exp(logw - m_t)
        a_inter = jnp.exp(m_inter - m_t)
        cnt = cnt_sc[i, h]
        nd = (lax.dot_general(v1, smat.astype(BF16), (((0,), (0,)), ((), ())), preferred_element_type=F32)
              + a_inter * _dot_nt(cnt.astype(BF16), qh))
        hv = nd[0:LDV] / jnp.maximum(jnp.abs(nd[LDV:2 * LDV]), jnp.exp(-m_t))
        hn = hv * lax.rsqrt(jnp.mean(hv * hv, axis=0, keepdims=True) + EPS) * ghb_ref[h, :, 0:L]
        h_ref[i, :, hs] = (og_ref[i, :, hs].astype(F32) * hn.T).astype(h_ref.dtype)
        m_new = m_t[:, L - 1:L]
        b_last = b_row[:, L - 1:L]
        decay = jnp.exp(m_prev + b_last - m_new)
        wk = (kh.astype(F32) * jnp.exp(g + (b_last - m_new))).astype(BF16)
        cnt_new = decay * cnt + lax.dot_general(v1, wk, (((0,), (0,)), ((), ())), preferred_element_type=F32)
        cnt_sc[i, h] = cnt_new
        m_out = jnp.where(lane == h, m_new, m_out)
    m_ref[i] = m_out


MLSTM_SEQS = 4


def _mlstm(q, k, v, og, small, gh, c0, n0, m0, *, n_chunks, layer):
    m = q.shape[0]
    b = c0.shape[1]
    L = m // (b * n_chunks)
    assert L <= LCHUNK and L % BF16_ROWS == 0
    ns = MLSTM_SEQS
    seq3 = lambda a: a.reshape(b, n_chunks * L, a.shape[-1])
    row = lambda w: pl.BlockSpec((ns, L, w), lambda i, c: (i, c, 0))
    st_c = pl.BlockSpec((ns, LH, LDK, LDV), lambda i, c: (i, 0, 0, 0))
    st_n = pl.BlockSpec((ns, LH, LDK), lambda i, c: (i, 0, 0))
    st_m = pl.BlockSpec((ns, 1, LANE), lambda i, c: (i, 0, 0))
    in_c = pl.BlockSpec((None, ns, LH, LDK, LDV), lambda i, c: (layer, i, 0, 0, 0))
    in_n = pl.BlockSpec((None, ns, LH, LDK), lambda i, c: (layer, i, 0, 0))
    h3, c_f, n_f, m_f = pl.pallas_call(
        functools.partial(_mlstm_kernel, n_chunks=n_chunks),
        out_shape=[
            jax.ShapeDtypeStruct((b, n_chunks * L, 512), BF16),
            jax.ShapeDtypeStruct((b, LH, LDK, LDV), F32),
            jax.ShapeDtypeStruct((b, LH, LDK), F32),
            jax.ShapeDtypeStruct((b, 1, LANE), F32),
        ],
        grid=(b // ns, n_chunks),
        in_specs=[row(512), row(512), row(512), row(512), row(LANE), _const_spec(gh.shape), in_c, in_n, st_m],
        out_specs=[row(512), st_c, st_n, st_m],
        scratch_shapes=[pltpu.VMEM((ns, LH, 2 * LDV, LDK), F32)],
        compiler_params=_cparams(("arbitrary", "arbitrary")),
        name="mlstm",
    )(seq3(q), seq3(k), seq3(v), seq3(og), seq3(small), gh, c0, n0, m0)
    return h3.reshape(m, 512), c_f, n_f, m_f


def _attn_kernel(q_ref, k_ref, vt_ref, o_ref, p_sc, *, tq):
    t = q_ref.shape[0]
    key = lax.broadcasted_iota(jnp.int32, (tq, tq), 0)
    qry = lax.broadcasted_iota(jnp.int32, (tq, tq), 1)
    for i in range(t // tq):
        r0, r1 = i * tq, (i + 1) * tq
        st = _dot_nt(k_ref[0:r1, :], q_ref[r0:r1, :])
        st_d = jnp.where(key <= qry, st[r0:], NEG_INF)
        m = jnp.max(st_d, axis=0, keepdims=True)
        if i > 0:
            m = jnp.maximum(m, jnp.max(st[:r0], axis=0, keepdims=True))
            p_sc[0:r0] = jnp.exp(st[:r0] - m).astype(BF16)
        p_sc[r0:r1] = jnp.exp(st_d - m).astype(BF16)
        ot = _dot(vt_ref[:, 0:r1], p_sc[0:r1])
        o_ref[r0:r1, :] = (ot / ot[A_V:A_V + 1, :]).T.astype(o_ref.dtype)


def _attn_prompt(qa, ka, vt, *, b, t, tq):
    blk = pl.BlockSpec((t, LANE), lambda i, h: (i, h))
    return pl.pallas_call(
        functools.partial(_attn_kernel, tq=tq),
        out_shape=jax.ShapeDtypeStruct(qa.shape, BF16),
        grid=(b, AH),
        in_specs=[blk, blk, pl.BlockSpec((LANE, t), lambda i, h: (i * AH + h, 0))],
        out_specs=blk,
        scratch_shapes=[pltpu.VMEM((t, tq), BF16)],
        compiler_params=_cparams(("arbitrary", "arbitrary")),
        name="attn_prompt",
    )(qa, ka, vt)


def _paged_kernel(pt_ref, ql_ref, qp_ref, cn_ref, kpn_ref, ksn_ref, ckv_hbm, kpe_hbm, ks_hbm,
                  o_ref, cbuf, kpbuf, ksbuf, sems, m_sc, l_sc, acc_sc, *, layer, n_per_step, n_chains, t_new):
    j = pl.program_id(1)
    steps = pl.num_programs(1)
    step = pl.program_id(0) * steps + j
    last = pl.num_programs(0) * steps - 1
    slot = step % PAGED_SLOTS
    ql = ql_ref[...]
    qp = qp_ref[...]
    nr = ql.shape[0]

    def page_copies(step_, n):
        slot_ = step_ % PAGED_SLOTS
        pg = pt_ref[jnp.minimum(step_, last) * n_per_step + n]
        return (pltpu.make_async_copy(ckv_hbm.at[layer, pg], cbuf.at[slot_, n], sems.at[slot_, 0]),
                pltpu.make_async_copy(kpe_hbm.at[layer, pg], kpbuf.at[slot_, n], sems.at[slot_, 1]),
                pltpu.make_async_copy(ks_hbm.at[layer, pg], ksbuf.at[slot_, n], sems.at[slot_, 2]))

    @pl.when(step == 0)
    def _():
        for ahead in range(PAGED_SLOTS - 1):
            for n in range(n_per_step):
                for cp in page_copies(ahead, n):
                    cp.start()

    for n in range(n_per_step):
        for cp in page_copies(step, n):
            cp.wait()

    @pl.when(j == 0)
    def _():
        cn = cn_ref[...].astype(BF16)
        s = (_dot_nt(ql, cn) + _dot(qp, kpn_ref[...].astype(BF16))) * ksn_ref[...]
        nk = cn.shape[0]
        r_t = lax.broadcasted_iota(jnp.int32, (nr, nk), 0) // AH
        key = lax.broadcasted_iota(jnp.int32, (nr, nk), 1)
        s = jnp.where((key <= r_t) & (key < t_new), s, NEG_INF)
        m0 = jnp.max(s, axis=-1, keepdims=True)
        p = jnp.exp(s - m0)
        m_sc[0] = m0
        l_sc[0] = jnp.sum(p, axis=-1, keepdims=True)
        acc_sc[0] = _dot(p.astype(BF16), cn)
        for c in range(1, n_chains):
            m_sc[c] = m0
            l_sc[c] = jnp.zeros_like(m0)
            acc_sc[c] = jnp.zeros((nr, A_KVR), F32)

    per_chain = n_per_step // n_chains
    cbs, ss = [], []
    for n in range(n_per_step):
        for cp in page_copies(step + PAGED_SLOTS - 1, n):
            cp.start()
        cb = cbuf[slot, n].astype(BF16)
        kpt = kpbuf[slot, n].astype(BF16)
        ks = ksbuf[slot, n]
        kst = jnp.concatenate([ks] * t_new, axis=0)
        cbs.append(cb)
        ss.append((_dot_nt(ql, cb) + _dot(qp, kpt)) * kst)
    finals = []
    for c in range(n_chains):
        sc = ss[c * per_chain:(c + 1) * per_chain]
        cc = cbs[c * per_chain:(c + 1) * per_chain]
        m_i = m_sc[c]
        m_new = jnp.maximum(m_i, jnp.max(functools.reduce(jnp.maximum, sc), axis=-1, keepdims=True))
        alpha = jnp.exp(m_i - m_new)
        ps = [jnp.exp(s - m_new) for s in sc]
        pv = functools.reduce(jnp.add, [_dot(p.astype(BF16), cb) for p, cb in zip(ps, cc)])
        l_i = alpha * l_sc[c] + jnp.sum(functools.reduce(jnp.add, ps), axis=-1, keepdims=True)
        acc = alpha * acc_sc[c] + pv
        m_sc[c] = m_new
        l_sc[c] = l_i
        acc_sc[c] = acc
        finals.append((m_new, l_i, acc))

    @pl.when(j == steps - 1)
    def _():
        m_f = functools.reduce(jnp.maximum, [f[0] for f in finals])
        ws = [jnp.exp(f[0] - m_f) for f in finals]
        l_f = functools.reduce(jnp.add, [w * f[1] for w, f in zip(ws, finals)])
        acc_f = functools.reduce(jnp.add, [w * f[2] for w, f in zip(ws, finals)])
        o_ref[...] = acc_f / l_f

    @pl.when(step == last)
    def _():
        for ahead in range(1, PAGED_SLOTS):
            for n in range(n_per_step):
                for cp in page_copies(last + ahead, n):
                    cp.wait()


def _attn_paged(page_table, qlat, qp, c_new, kp_new_t, ks_new_t, cache_ckv, cache_kpe_t, cache_ks_t, *, layer, t_new,
                n_per_step, n_chains):
    bs, nr, _ = qlat.shape
    n_pages = page_table.shape[1]
    nk = c_new.shape[1]
    steps = n_pages // n_per_step
    pt_flat = page_table.reshape(-1)

    per_b = lambda s1, s2: pl.BlockSpec((None, s1, s2), lambda b, j, pt: (b, 0, 0))
    hbm = pl.BlockSpec(memory_space=pl.ANY)
    grid_spec = pltpu.PrefetchScalarGridSpec(
        num_scalar_prefetch=1,
        grid=(bs, steps),
        in_specs=[per_b(nr, A_KVR), per_b(nr, A_ROPE), per_b(nk, A_KVR), per_b(A_ROPE, nk), per_b(nr, nk),
                  hbm, hbm, hbm],
        out_specs=per_b(nr, A_KVR),
        scratch_shapes=[
            pltpu.VMEM((PAGED_SLOTS, n_per_step, PAGE, A_KVR), F32),
            pltpu.VMEM((PAGED_SLOTS, n_per_step, A_ROPE, PAGE), F32),
            pltpu.VMEM((PAGED_SLOTS, n_per_step, AH, PAGE), F32),
            pltpu.SemaphoreType.DMA((PAGED_SLOTS, 3)),
            pltpu.VMEM((n_chains, nr, 1), F32), pltpu.VMEM((n_chains, nr, 1), F32),
            pltpu.VMEM((n_chains, nr, A_KVR), F32),
        ],
    )
    return pl.pallas_call(
        functools.partial(_paged_kernel, layer=layer, n_per_step=n_per_step, n_chains=n_chains, t_new=t_new),
        out_shape=jax.ShapeDtypeStruct((bs, nr, A_KVR), F32),
        grid_spec=grid_spec,
        compiler_params=_cparams(("arbitrary", "arbitrary")),
        name="attn_paged",
    )(pt_flat, qlat, qp, c_new, kp_new_t, ks_new_t, cache_ckv, cache_kpe_t, cache_ks_t)


def _uv_kernel(x_ref, w_ref, o_ref):
    o_ref[...] = _dot(x_ref[...].astype(BF16), w_ref[...]).astype(o_ref.dtype)


def _uv_proj(olat_h, wuvp):
    _, m, _ = olat_h.shape
    return pl.pallas_call(
        _uv_kernel,
        out_shape=jax.ShapeDtypeStruct((m, AH * LANE), BF16),
        grid=(AH,),
        in_specs=[pl.BlockSpec((None, m, A_KVR), lambda h: (h, 0, 0)),
                  pl.BlockSpec((None, A_KVR, LANE), lambda h: (h, 0, 0))],
        out_specs=pl.BlockSpec((m, LANE), lambda h: (0, h)),
        compiler_params=_cparams(("arbitrary",)),
        name="uv_proj",
    )(olat_h, wuvp)


def _merge_kernel(x_ref, g1_ref, oa_ref, ob_ref, oc_ref, gates_ref, wa_ref, wb_ref, wc_ref, wo_ref, o_ref):
    mix = gates_ref[:, 0:D].astype(F32) * _dot(oa_ref[...], wa_ref[...])
    mix += gates_ref[:, D:2 * D].astype(F32) * _dot(ob_ref[...], wb_ref[...])
    mix += gates_ref[:, 2 * D:3 * D].astype(F32) * _dot(oc_ref[...], wc_ref[...])
    o_ref[...] = x_ref[...] + g1_ref[...] * _dot(mix.astype(BF16), wo_ref[...])


def _mod_spec(tm, rows_per_mod):
    if rows_per_mod is None:
        return pl.BlockSpec((tm, D), lambda i: (i, 0))
    tpb = rows_per_mod // tm
    return pl.BlockSpec((None, 1, D), lambda i: (i // tpb, 0, 0))


def _merge(x2, g1, oa, ob, oc, gates, wa, wb, wc, wo, *, tm, rows_per_mod):
    m = x2.shape[0]
    row = lambda w: pl.BlockSpec((tm, w), lambda i: (i, 0))
    return pl.pallas_call(
        _merge_kernel,
        out_shape=jax.ShapeDtypeStruct((m, D), F32),
        grid=(m // tm,),
        in_specs=[row(D), _mod_spec(tm, rows_per_mod), row(512), row(512), row(AH * LANE), row(3 * D),
                  _const_spec(wa.shape), _const_spec(wb.shape), _const_spec(wc.shape), _const_spec(wo.shape)],
        out_specs=row(D),
        compiler_params=_cparams(("arbitrary",)),
        name="merge",
    )(x2, g1, oa, ob, oc, gates, wa, wb, wc, wo)


def _moe_kernel(x_ref, sc_ref, sh_ref, g2_ref, gn_ref, wr1_ref, wr2_ref, br_ref, wg_ref, wu_ref, wd_ref, o_ref):
    tm = x_ref.shape[0]
    x = x_ref[...]
    h2 = _rms(x, gn_ref[...]) * (1.0 + sc_ref[...]) + sh_ref[...]
    hi = h2.astype(BF16)
    lo = (h2 - hi.astype(F32)).astype(BF16)
    r = _dot(hi, wr1_ref[...]) + _dot(lo, wr2_ref[...])
    rt = r.T
    logits = rt[0:32] + rt[32:64]
    scores = _sigmoid(logits)
    sel = scores + br_ref[...]
    e = [sel[8 * p:8 * p + 8] for p in range(GRP_E)]
    sc = [scores[8 * p:8 * p + 8] for p in range(GRP_E)]
    a, b = jnp.maximum(e[0], e[1]), jnp.minimum(e[0], e[1])
    c, d = jnp.maximum(e[2], e[3]), jnp.minimum(e[2], e[3])
    top1 = jnp.maximum(a, c)
    top2 = jnp.maximum(jnp.minimum(a, c), jnp.maximum(b, d))
    rowi = lax.broadcasted_iota(jnp.int32, (8, tm), 0).astype(F32)
    gs = jnp.where(rowi < N_GRP, top1 + top2, NEG_INF)
    gmax = jnp.max(gs, axis=0, keepdims=True)
    gidx = jnp.min(jnp.where(gs == gmax, rowi, 8.0), axis=0, keepdims=True)
    oh = rowi == gidx
    s = [jnp.sum(jnp.where(oh, e[p], 0.0), axis=0, keepdims=True) for p in range(GRP_E)]
    w = [jnp.sum(jnp.where(oh, sc[p], 0.0), axis=0, keepdims=True) for p in range(GRP_E)]

    def first_argmax(vals):
        best = jnp.maximum(jnp.maximum(vals[0], vals[1]), jnp.maximum(vals[2], vals[3]))
        return jnp.where(vals[0] == best, 0, jnp.where(vals[1] == best, 1, jnp.where(vals[2] == best, 2, 3)))

    i1 = first_argmax(s)
    i2 = first_argmax([jnp.where(i1 == p, NEG_INF, s[p]) for p in range(GRP_E)])
    w1 = sum(jnp.where(i1 == p, w[p], 0.0) for p in range(GRP_E))
    w2 = sum(jnp.where(i2 == p, w[p], 0.0) for p in range(GRP_E))
    den = w1 + w2
    w1 = w1 / den
    w2 = w2 / den
    comb = [jnp.where(oh, jnp.where(i1 == p, w1, 0.0) + jnp.where(i2 == p, w2, 0.0), 0.0) for p in range(GRP_E)]
    comb_t = jnp.concatenate(comb + [jnp.zeros((LANE - 8 * GRP_E, tm), F32)], axis=0)
    combine = comb_t.T

    acc = jnp.zeros((tm, D), F32)
    for g in range(N_GRP):
        acts = []
        for p in range(GRP_E):
            ex = g * GRP_E + p
            hg = _dot(hi, wg_ref[ex])
            hu = _dot(hi, wu_ref[ex])
            cw = combine[:, 8 * p + g:8 * p + g + 1]
            acts.append((hg * _sigmoid(hg) * hu * cw).astype(BF16))
        acc += _dot(jnp.concatenate(acts, axis=1), wd_ref[g])
    o_ref[...] = x + g2_ref[...] * acc


def _moe(x2, sc, sh, g2, gn, wr1, wr2, br, wg, wu, wd, *, tm, rows_per_mod):
    m = x2.shape[0]
    row = lambda w: pl.BlockSpec((tm, w), lambda i: (i, 0))
    ms = _mod_spec(tm, rows_per_mod)
    return pl.pallas_call(
        _moe_kernel,
        out_shape=jax.ShapeDtypeStruct((m, D), F32),
        grid=(m // tm,),
        in_specs=[row(D), ms, ms, ms, _const_spec(gn.shape), _const_spec(wr1.shape), _const_spec(wr2.shape),
                  _const_spec(br.shape), _const_spec(wg.shape), _const_spec(wu.shape), _const_spec(wd.shape)],
        out_specs=row(D),
        compiler_params=_cparams(("arbitrary",)),
        name="moe",
    )(x2, sc, sh, g2, gn, wr1, wr2, br, wg, wu, wd)


def _head_block(nope, r1, r2):
    pad = jnp.zeros(nope.shape[:-1] + (LANE - A_QK,), nope.dtype)
    blk = jnp.concatenate([nope, r1, r2, pad], axis=-1)
    return blk.reshape(blk.shape[:-2] + (AH * LANE,))


def _prep_layer(l, w_in, b_in, w_gate, b_gate, g_norm1, g_norm2, g_sgu, g_mlstm, g_cq, w_q_up, g_ckv, w_uk, w_uv,
                g_qn, g_kn, w_br, w_o, w_exp_gate, w_exp_up, w_exp_down):
    half = A_ROPE // 2
    wi, bi = w_in[l], b_in[l]

    def cat_cols(a, gate):
        z = lambda n: jnp.zeros(a.shape[:-1] + (n,), a.dtype)
        small = jnp.concatenate([z(SM_KR), a[..., 3592:3624], a[..., 3072:3080], z(LANE - SM_F - LH)], axis=-1)
        return jnp.concatenate([a[..., :3072], a[..., 3080:3592], small, gate], axis=-1)

    lw = {}
    lw["wcat"] = cat_cols(wi, w_gate[l]).astype(BF16)
    lw["bcat"] = cat_cols(bi, b_gate[l]).reshape(1, N_CAT)
    lw["g1"] = g_norm1[l].reshape(1, D)
    lw["g2"] = g_norm2[l].reshape(1, D)
    lw["gsgu"] = g_sgu[l].reshape(1, SGU_W)
    lw["gh"] = jnp.broadcast_to(g_mlstm[l][:, :, None], (LH, LDV, LCHUNK))
    wq = w_q_up[l]
    nope, r1, r2 = wq[..., :A_NOPE], wq[..., A_NOPE:A_NOPE + half], wq[..., A_NOPE + half:]
    lw["wq"] = _head_block(nope, r1, r2).astype(BF16)
    lw["wqr"] = _head_block(jnp.zeros_like(nope), r2, r1).astype(BF16)
    gq = jnp.broadcast_to(g_qn[l], (AH, A_QK))
    gn_, g1_, g2_ = gq[:, :A_NOPE], gq[:, A_NOPE:A_NOPE + half], gq[:, A_NOPE + half:]
    lw["gq"] = _head_block(gn_, g1_, g2_).reshape(1, AH * LANE)
    lw["gqr"] = _head_block(jnp.zeros_like(gn_), g2_, g1_).reshape(1, AH * LANE)
    zr = jnp.zeros((A_KVR, AH, half), F32)
    lw["wuk"] = _head_block(w_uk[l], zr, zr).astype(BF16)
    uv = jnp.concatenate([w_uv[l], jnp.zeros((A_KVR, AH, LANE - A_V), F32)], axis=-1)
    lw["wuv"] = uv.reshape(A_KVR, AH * LANE).T.astype(BF16)
    lw["wuvp"] = jnp.transpose(uv, (1, 0, 2)).astype(BF16)
    lw["vone"] = jnp.tile((jnp.arange(LANE) == A_V).astype(F32), AH).reshape(AH * LANE, 1)
    lw["gcq"] = g_cq[l].reshape(1, A_QR)
    lw["gckv"] = g_ckv[l].reshape(1, A_KVR)
    gk = g_kn[l]
    lw["gkn"] = jnp.concatenate([gk[:A_NOPE], jnp.zeros((LANE - A_NOPE,), F32)]).reshape(1, LANE)
    lw["gkp"] = jnp.concatenate([jnp.zeros((SM_KR,), F32), gk[A_NOPE:], jnp.zeros((LANE - A_QK,), F32)]).reshape(1, LANE)
    ukt = jnp.transpose(w_uk[l], (1, 2, 0))
    lw["wukt"] = jnp.concatenate([ukt, jnp.zeros((AH, LANE - A_NOPE, A_KVR), F32)], axis=1).astype(BF16)
    lw["wa"] = w_br[l, 0].astype(BF16)
    lw["wb"] = w_br[l, 1].astype(BF16)
    wc = w_br[l, 2].reshape(AH, A_V, D)
    lw["wc"] = jnp.concatenate([wc, jnp.zeros((AH, LANE - A_V, D), F32)], axis=1).reshape(AH * LANE, D).astype(BF16)
    lw["wo"] = w_o[l].astype(BF16)
    lw["wg"] = w_exp_gate[l].astype(BF16)
    lw["wu"] = w_exp_up[l].astype(BF16)
    lw["wd"] = w_exp_down[l].reshape(N_GRP, GRP_E * D_EXP, D).astype(BF16)
    return lw


def _prep_router(w_router, b_router):
    wr = w_router.reshape(D, N_GRP, GRP_E).transpose(0, 2, 1)
    wr = jnp.concatenate([wr, jnp.zeros((D, GRP_E, 8 - N_GRP), F32)], axis=-1).reshape(D, 8 * GRP_E)
    hi = wr.astype(BF16)
    lo = (wr - hi.astype(F32)).astype(BF16)
    z = lambda n: jnp.zeros((D, n), BF16)
    wr1 = jnp.concatenate([hi, lo, z(LANE - 64)], axis=1)
    wr2 = jnp.concatenate([hi, z(LANE - 32)], axis=1)
    br = b_router.reshape(N_GRP, GRP_E).T
    br = jnp.concatenate([br, jnp.zeros((GRP_E, 8 - N_GRP), F32)], axis=-1).reshape(8 * GRP_E, 1)
    return wr1, wr2, br


def _rope_tables(pos):
    half = A_ROPE // 2
    freqs = ROPE_BASE ** (-jnp.arange(half, dtype=F32) / half)
    ang = pos[:, None] * freqs[None, :]
    cos, sin = jnp.cos(ang), jnp.sin(ang)
    n = pos.shape[0]
    cos_t = jnp.concatenate([jnp.ones((n, A_NOPE), F32), cos, cos, jnp.zeros((n, LANE - A_QK), F32)], axis=1)
    sin_t = jnp.concatenate([jnp.zeros((n, A_NOPE), F32), -sin, sin, jnp.zeros((n, LANE - A_QK), F32)], axis=1)
    return cos_t, sin_t


def _mix_weights(w_spatial_l, b_spatial_l, chunk, reps):
    w = jnp.where(jnp.tril(jnp.ones((chunk, chunk), dtype=bool))[None], w_spatial_l[:, :chunk, :chunk], 0.0)
    if reps > 1:
        idx = jnp.arange(reps * chunk)
        pos = jax.nn.one_hot(idx % chunk, chunk, dtype=F32)
        same = (idx[:, None] // chunk) == (idx[None, :] // chunk)
        w = jnp.where(same[None], jnp.einsum("rt,gts,cs->grc", pos, w, pos, precision=lax.Precision.HIGHEST), 0.0)
    bias = jnp.tile(b_spatial_l[:, :chunk].T, (reps, 1))
    bias = jnp.repeat(bias, SGU_C, axis=1)
    return w.astype(BF16), bias


def _layer_prompt(x2, mod, lw, router, tables, wmix, bmix, b, t):
    m = b * t
    tm = 512
    sh1, sc1, g1, sh2, sc2, g2 = [a.reshape(b, 1, D) for a in jnp.split(mod, 6, axis=-1)]
    oa, q, k, v, og, zc, small, gates = _proj(x2, sc1, sh1, lw["wcat"], lw["bcat"], lw["g1"], lw["gsgu"], wmix, bmix,
                                              tm=tm, rows_per_mod=t, emit_v=False)
    ckv, kpe, ksc, qa, ka, va = _mla_prep(zc, small, tables[0], tables[1], lw, tm=tm, rows_per_pos=t, sample=False)
    n_chunks = t // LCHUNK
    c0 = jnp.zeros((b, LH, LDK, LDV), F32)
    n0 = jnp.zeros((b, LH, LDK), F32)
    m0 = jnp.zeros((b, 1, LANE), F32)
    ob, c_f, n_f, m_f = _mlstm(q, k, v, og, small, lw["gh"], c0[None], n0[None], m0, n_chunks=n_chunks, layer=0)
    oc = _attn_prompt(qa, ka, va, b=b, t=t, tq=min(t, 512))
    x1 = _merge(x2, g1, oa, ob, oc, gates, lw["wa"], lw["wb"], lw["wc"], lw["wo"], tm=tm, rows_per_mod=t)
    x_out = _moe(x1, sc2, sh2, g2, lw["g2"], *router, lw["wg"], lw["wu"], lw["wd"], tm=tm, rows_per_mod=t)
    state = (ckv.reshape(b, t, A_KVR), kpe.reshape(b, t, A_ROPE), ksc.reshape(b, t, AH), c_f, n_f, m_f[:, 0, :LH])
    return x_out, state


def _layer_sample(x2, mod, lw, router, tables, wmix, bmix, b, t, st_c, st_n, st_m, caches, page_table, layer):
    m = b * t
    tm = m
    mods = [jnp.repeat(a, t, axis=0) for a in jnp.split(mod, 6, axis=-1)]
    sh1, sc1, g1, sh2, sc2, g2 = mods
    oa, q, k, v, og, zc, small, gates, v_sgu = _proj(x2, sc1, sh1, lw["wcat"], lw["bcat"], lw["g1"], lw["gsgu"], wmix,
                                                     bmix, tm=tm, rows_per_mod=None, emit_v=True)
    ckv, kpe, ksc, qa, qlat = _mla_prep(zc, small, tables[0], tables[1], lw, tm=tm, rows_per_pos=None, sample=True)

    lpad = -(-t // BF16_ROWS) * BF16_ROWS

    def pad_rows(a, fill=None):
        a3 = a.reshape(b, t, a.shape[-1])
        if fill is None:
            padv = jnp.zeros((b, lpad - t, a.shape[-1]), a.dtype)
        else:
            padv = jnp.broadcast_to(fill, (b, lpad - t, a.shape[-1])).astype(a.dtype)
        return jnp.concatenate([a3, padv], axis=1).reshape(b * lpad, a.shape[-1])

    lane = jnp.arange(LANE)
    inert = jnp.where((lane >= SM_I) & (lane < SM_I + LH), -1e30, jnp.where((lane >= SM_F) & (lane < SM_F + LH), 1e30, 0.0))
    m0 = jnp.concatenate([st_m, jnp.zeros((b, LANE - LH), F32)], axis=1).reshape(b, 1, LANE)
    ob_p, c_f, n_f, m_f = _mlstm(pad_rows(q), pad_rows(k), pad_rows(v), pad_rows(og), pad_rows(small, inert.astype(F32)),
                                 lw["gh"], st_c, st_n, m0, n_chunks=1, layer=layer)
    ob = ob_p.reshape(b, lpad, 512)[:, :t].reshape(m, 512)

    nr = t * AH
    nk = 8
    ql3 = qlat.reshape(b, nr, A_KVR).astype(BF16)
    qp3 = qa.reshape(b, t, AH, LANE)[..., A_NOPE:A_QK].reshape(b, nr, A_ROPE).astype(BF16)
    padk = lambda a: jnp.concatenate([a.reshape(b, t, -1), jnp.zeros((b, nk - t, a.shape[-1]), F32)], axis=1)
    ks_t = jnp.transpose(padk(ksc), (0, 2, 1))
    ks_t = jnp.tile(ks_t, (1, t, 1))
    kp_t = jnp.transpose(padk(kpe), (0, 2, 1))
    olat = _attn_paged(page_table, ql3, qp3, padk(ckv), kp_t, ks_t, *caches, layer=layer, t_new=t, n_per_step=32,
                       n_chains=4)
    olat_h = jnp.transpose(olat.reshape(b, t, AH, A_KVR), (2, 0, 1, 3)).reshape(AH, m, A_KVR)
    oc = _uv_proj(olat_h, lw["wuvp"])

    x1 = _merge(x2, g1, oa, ob, oc, gates, lw["wa"], lw["wb"], lw["wc"], lw["wo"], tm=tm, rows_per_mod=None)
    x_out = _moe(x1, sc2, sh2, g2, lw["g2"], *router, lw["wg"], lw["wu"], lw["wd"], tm=tm, rows_per_mod=None)
    state = (ckv.reshape(b, t, A_KVR), kpe.reshape(b, t, A_ROPE), ksc.reshape(b, t, AH), c_f, n_f, m_f[:, 0, :LH],
             v_sgu.reshape(b, t, SGU_W))
    return x_out, state


def kernel(x_prompt, x_sample, c_prompt, c_sample, cache_ckv, cache_kpe, cache_kscale, page_table, state_C, state_n, state_m, w_ada, b_ada, g_norm1, g_norm2, w_in, b_in, g_sgu, w_spatial, b_spatial, g_mlstm, g_cq, w_q_up, g_ckv, w_uk, w_uv, g_qn, g_kn, w_br, w_gate, b_gate, w_o, w_router, b_router, w_exp_gate, w_exp_up, w_exp_down):
    bp, tp, _ = x_prompt.shape
    bs, ts, _ = x_sample.shape
    depth = w_in.shape[0]
    past_len = page_table.shape[1] * PAGE

    mod_all = _ada(jnp.concatenate([c_prompt, c_sample], axis=0), w_ada, b_ada)
    router = _prep_router(w_router, b_router)
    tab_p = _rope_tables(jnp.arange(tp, dtype=F32))
    pos_s = jnp.tile(jnp.arange(ts, dtype=F32) + past_len, bs)
    tab_s = _rope_tables(pos_s)
    caches = (cache_ckv, jnp.swapaxes(cache_kpe, 2, 3), jnp.swapaxes(cache_kscale, 2, 3))

    xp = x_prompt.reshape(bp * tp, D)
    xs = x_sample.reshape(bs * ts, D)
    st_p, st_s = [], []
    for l in range(depth):
        lw = _prep_layer(l, w_in, b_in, w_gate, b_gate, g_norm1, g_norm2, g_sgu, g_mlstm, g_cq, w_q_up, g_ckv, w_uk,
                         w_uv, g_qn, g_kn, w_br, w_o, w_exp_gate, w_exp_up, w_exp_down)
        chunk_p = min(tp, SGU_C)
        wmix_p, bmix_p = _mix_weights(w_spatial[l], b_spatial[l], chunk_p, 1)
        wmix_s, bmix_s = _mix_weights(w_spatial[l], b_spatial[l], ts, bs)
        xp, sp = _layer_prompt(xp, mod_all[l, :bp], lw, router, tab_p, wmix_p, bmix_p, bp, tp)
        xs, ss = _layer_sample(xs, mod_all[l, bp:], lw, router, tab_s, wmix_s, bmix_s, bs, ts, state_C, state_n,
                               state_m[l], caches, page_table, l)
        st_p.append(sp)
        st_s.append(ss)
    stack = lambda sts, i: jnp.stack([s[i] for s in sts])
    return (xp.reshape(bp, tp, D), xs.reshape(bs, ts, D),
            stack(st_p, 0), stack(st_p, 1), stack(st_p, 2), stack(st_p, 3), stack(st_p, 4), stack(st_p, 5),
            stack(st_s, 0), stack(st_s, 1), stack(st_s, 2), stack(st_s, 3), stack(st_s, 4), stack(st_s, 5),
            stack(st_s, 6))
```
